```python
import math
import jax, jax.numpy as jnp
from jax import lax
import numpy as np

D_MODEL = 1024
BATCH = 8
SEQ = 2048
DEPTH = 4
DEC_BATCH = 32
DEC_SEQ = 64
PAST_LEN = 2048

CHUNK = 64
N_A_LAYERS = DEPTH // 2
N_B_LAYERS = DEPTH - N_A_LAYERS
MLSTM_HEADS = 8
MLSTM_QK_DIM = D_MODEL // (2 * MLSTM_HEADS)
MLSTM_V_DIM = D_MODEL // MLSTM_HEADS
MLSTM_IN_WIDTH = 2 * MLSTM_HEADS * MLSTM_QK_DIM + MLSTM_HEADS * MLSTM_V_DIM + D_MODEL + 2 * MLSTM_HEADS
GATE_SOFTCAP = 15.0
SB_HEADS = 8
SB_HEAD_DIM = D_MODEL // SB_HEADS
SB_QUERY_BLOCK = 128
D_FF = 2816
CONV_WIDTH = 3
NORM_EPS = 1e-6
NEG_BIG = -1e30

kernel_name = "yoco_mlstm_stickbreaking_convffn_step"


def rmsnorm(x, g):
    x32 = x.astype(jnp.float32)
    y = x32 * lax.rsqrt(jnp.mean(x32 * x32, axis=-1, keepdims=True) + NORM_EPS)
    return (y * g.astype(jnp.float32)).astype(x.dtype)


def softcap(x, cap):
    return cap * jnp.tanh(x / cap)


def mlstm_scan(q, k, v, li, lf, c0, n0, m0):
    B, H, T, DK = q.shape
    DV = v.shape[-1]
    L = min(CHUNK, T)
    NC = T // L

    def to_chunks(a):
        return jnp.moveaxis(a.reshape(a.shape[:2] + (NC, L) + a.shape[3:]), 2, 0)

    xs = tuple(to_chunks(a.astype(jnp.float32)) for a in (q, k, v, li, lf))
    causal = jnp.tril(jnp.ones((L, L), dtype=bool))

    def step(carry, chunk):
        c, n, m = carry
        qc, kc, vc, lic, lfc = chunk
        b = jnp.cumsum(lfc, axis=-1)
        log_d = jnp.where(causal, b[..., :, None] - b[..., None, :] + lic[..., None, :], -jnp.inf)
        m_inter = b + m[..., None]
        m_t = jnp.maximum(m_inter, jnp.max(log_d, axis=-1))
        w_inter = jnp.exp(m_inter - m_t)
        s = jnp.einsum('bhtd,bhsd->bhts', qc, kc) * jnp.exp(log_d - m_t[..., None])
        num = jnp.einsum('bhts,bhsv->bhtv', s, vc) + w_inter[..., None] * jnp.einsum('bhtd,bhdv->bhtv', qc, c)
        den = jnp.sum(s, axis=-1) + w_inter * jnp.einsum('bhtd,bhd->bht', qc, n)
        h = num / jnp.maximum(jnp.abs(den), jnp.exp(-m_t))[..., None]
        m_new = m_t[..., -1]
        decay = jnp.exp(b[..., -1] + m - m_new)
        w_in = jnp.exp(b[..., -1:] - b + lic - m_new[..., None])
        c_new = decay[..., None, None] * c + jnp.einsum('bhsd,bhsv->bhdv', kc * w_in[..., None], vc)
        n_new = decay[..., None] * n + jnp.einsum('bhsd,bhs->bhd', kc, w_in)
        return (c_new, n_new, m_new), h

    carry0 = (c0.astype(jnp.float32), n0.astype(jnp.float32), m0.astype(jnp.float32))
    (c, n, m), h = lax.scan(step, carry0, xs)
    h = jnp.moveaxis(h, 0, 2).reshape(B, H, T, DV)
    return h, c, n, m


def mlstm_layer(x, g_norm, w_in, b_gates, g_out, w_out, c0, n0, m0):
    B, T, _ = x.shape
    H, DK, DV = MLSTM_HEADS, MLSTM_QK_DIM, MLSTM_V_DIM
    proj = rmsnorm(x, g_norm) @ w_in
    s1 = H * DK
    s2 = 2 * H * DK
    s3 = s2 + H * DV
    s4 = s3 + D_MODEL
    q = proj[..., :s1].reshape(B, T, H, DK).transpose(0, 2, 1, 3) * (DK ** -0.5)
    k = proj[..., s1:s2].reshape(B, T, H, DK).transpose(0, 2, 1, 3)
    v = proj[..., s2:s3].reshape(B, T, H, DV).transpose(0, 2, 1, 3)
    o = proj[..., s3:s4]
    gates = softcap(proj[..., s4:].astype(jnp.float32) + b_gates.astype(jnp.float32), GATE_SOFTCAP)
    li = gates[..., :H].transpose(0, 2, 1)
    lf = jax.nn.log_sigmoid(gates[..., H:]).transpose(0, 2, 1)
    h, c, n, m = mlstm_scan(q, k, v, li, lf, c0, n0, m0)
    h = h.transpose(0, 2, 1, 3)
    h = h * lax.rsqrt(jnp.mean(h * h, axis=-1, keepdims=True) + NORM_EPS)
    h = h.reshape(B, T, D_MODEL) * g_out.astype(jnp.float32) * jax.nn.sigmoid(o.astype(jnp.float32))
    out = h.astype(x.dtype) @ w_out
    return out, c.astype(x.dtype), n.astype(x.dtype), m.astype(x.dtype)


def stick_breaking_attention(q, k, v):
    B, Tq, H, DH = q.shape
    Tk = k.shape[1]
    QB = min(SB_QUERY_BLOCK, Tq)
    NB = Tq // QB
    q_blocks = jnp.moveaxis(q.astype(jnp.float32).reshape(B, NB, QB, H, DH), 1, 0)
    k32 = k.astype(jnp.float32)
    v32 = v.astype(jnp.float32)
    k_pos = jnp.arange(Tk)
    scale = DH ** -0.5

    def block(args):
        q_blk, i = args
        q_pos = (Tk - Tq) + i * QB + jnp.arange(QB)
        z = jnp.einsum('bqhd,bkhd->bhqk', q_blk, k32) * scale
        earlier = k_pos[None, :] < q_pos[:, None]
        log_beta = jnp.where(earlier, jax.nn.log_sigmoid(z), -jnp.inf)
        log_keep = jnp.where(earlier, jax.nn.log_sigmoid(-z), 0.0)
        log_keep_after = lax.cumsum(log_keep, axis=3, reverse=True) - log_keep
        a = jnp.exp(log_beta + log_keep_after)
        return jnp.einsum('bhqk,bkhd->bqhd', a, v32)

    out = lax.map(block, (q_blocks, jnp.arange(NB)))
    return jnp.moveaxis(out, 0, 1).reshape(B, Tq, H, DH)


def conv_ffn(x, g_norm, w_up, w_conv, b_conv, w_down, prev):
    T = x.shape[1]
    u = rmsnorm(x, g_norm) @ w_up
    a = u[..., :D_FF]
    val = u[..., D_FF:]
    ext = jnp.concatenate([prev.astype(a.dtype), a], axis=1)
    acc = b_conv
    for i in range(CONV_WIDTH):
        acc = acc + w_conv[i] * ext[:, i:i + T]
    h = jax.nn.gelu(acc.astype(jnp.float32)) * val.astype(jnp.float32)
    return h.astype(x.dtype) @ w_down, ext[:, -(CONV_WIDTH - 1):]


def trunk(x, c0, n0, m0, conv0, k_past, v_past,
          g_mix_norm, w_mlstm_in, b_mlstm_gates, g_mlstm_out, w_mlstm_out,
          g_kv_norm, w_kv, w_sb_q, w_sb_o,
          g_ffn_norm, w_ffn_up, w_ffn_conv, b_ffn_conv, w_ffn_down, g_final):
    B, T, _ = x.shape
    new_c, new_n, new_m, new_conv = [], [], [], []
    k_all = v_all = k_new = v_new = None
    for l in range(DEPTH):
        if l < N_A_LAYERS:
            a, c, n, m = mlstm_layer(x, g_mix_norm[l], w_mlstm_in[l], b_mlstm_gates[l], g_mlstm_out[l],
                                     w_mlstm_out[l], c0[l], n0[l], m0[l])
            x = x + a
            new_c.append(c)
            new_n.append(n)
            new_m.append(m)
        else:
            j = l - N_A_LAYERS
            q = (rmsnorm(x, g_mix_norm[l]) @ w_sb_q[j]).reshape(B, T, SB_HEADS, SB_HEAD_DIM)
            att = stick_breaking_attention(q, k_all, v_all).reshape(B, T, D_MODEL).astype(x.dtype)
            x = x + att @ w_sb_o[j]
        f, conv_rows = conv_ffn(x, g_ffn_norm[l], w_ffn_up[l], w_ffn_conv[l], b_ffn_conv[l], w_ffn_down[l], conv0[l])
        x = x + f
        new_conv.append(conv_rows)
        if l == N_A_LAYERS - 1:
            kv = rmsnorm(x, g_kv_norm) @ w_kv
            k_new = kv[..., :D_MODEL].reshape(B, T, SB_HEADS, SB_HEAD_DIM)
            v_new = kv[..., D_MODEL:].reshape(B, T, SB_HEADS, SB_HEAD_DIM)
            k_all = jnp.concatenate([k_past.astype(x.dtype), k_new], axis=1)
            v_all = jnp.concatenate([v_past.astype(x.dtype), v_new], axis=1)
    y = rmsnorm(x, g_final)
    return y, jnp.stack(new_c), jnp.stack(new_n), jnp.stack(new_m), jnp.stack(new_conv), k_new, v_new


def setup_inputs(seed: int = 0) -> dict:
    key = jax.random.key(seed)
    ks = jax.random.split(key, 24)
    f32 = jnp.float32
    D = D_MODEL

    def nrm(k, shape, scale):
        return jax.random.normal(k, shape, f32) * scale

    b_gates = jnp.concatenate([nrm(ks[10], (N_A_LAYERS, MLSTM_HEADS), 0.1),
                               3.0 + nrm(ks[11], (N_A_LAYERS, MLSTM_HEADS), 0.1)], axis=-1)
    return {
        'x_prompt': nrm(ks[0], (BATCH, SEQ, D), 1.0),
        'x_sample': nrm(ks[1], (DEC_BATCH, DEC_SEQ, D), 1.0),
        'cache_k': nrm(ks[2], (DEC_BATCH, PAST_LEN, SB_HEADS, SB_HEAD_DIM), 1.0),
        'cache_v': nrm(ks[3], (DEC_BATCH, PAST_LEN, SB_HEADS, SB_HEAD_DIM), 1.0),
        'state_mlstm_c': nrm(ks[4], (N_A_LAYERS, DEC_BATCH, MLSTM_HEADS, MLSTM_QK_DIM, MLSTM_V_DIM), 0.1),
        'state_mlstm_n': nrm(ks[5], (N_A_LAYERS, DEC_BATCH, MLSTM_HEADS, MLSTM_QK_DIM), 0.1),
        'state_mlstm_m': nrm(ks[6], (N_A_LAYERS, DEC_BATCH, MLSTM_HEADS), 0.5),
        'state_ffn_conv': nrm(ks[7], (DEPTH, DEC_BATCH, CONV_WIDTH - 1, D_FF), 1.0),
        'g_mix_norm': 1.0 + nrm(ks[8], (DEPTH, D), 0.02),
        'w_mlstm_in': nrm(ks[9], (N_A_LAYERS, D, MLSTM_IN_WIDTH), D ** -0.5),
        'b_mlstm_gates': b_gates,
        'g_mlstm_out': 1.0 + nrm(ks[12], (N_A_LAYERS, D), 0.02),
        'w_mlstm_out': nrm(ks[13], (N_A_LAYERS, D, D), D ** -0.5),
        'g_kv_norm': 1.0 + nrm(ks[14], (D,), 0.02),
        'w_kv': nrm(ks[15], (D, 2 * D), D ** -0.5),
        'w_sb_q': nrm(ks[16], (N_B_LAYERS, D, D), D ** -0.5),
        'w_sb_o': nrm(ks[17], (N_B_LAYERS, D, D), D ** -0.5),
        'g_ffn_norm': 1.0 + nrm(ks[18], (DEPTH, D), 0.02),
        'w_ffn_up': nrm(ks[19], (DEPTH, D, 2 * D_FF), D ** -0.5),
        'w_ffn_conv': nrm(ks[20], (DEPTH, CONV_WIDTH, D_FF), CONV_WIDTH ** -0.5),
        'b_ffn_conv': nrm(ks[21], (DEPTH, D_FF), 0.02),
        'w_ffn_down': nrm(ks[22], (DEPTH, D_FF, D), D_FF ** -0.5),
        'g_final': 1.0 + nrm(ks[23], (D,), 0.02),
    }


def reference(x_prompt, x_sample, cache_k, cache_v, state_mlstm_c, state_mlstm_n, state_mlstm_m, state_ffn_conv,
              g_mix_norm, w_mlstm_in, b_mlstm_gates, g_mlstm_out, w_mlstm_out,
              g_kv_norm, w_kv, w_sb_q, w_sb_o,
              g_ffn_norm, w_ffn_up, w_ffn_conv, b_ffn_conv, w_ffn_down, g_final):
    weights = (g_mix_norm, w_mlstm_in, b_mlstm_gates, g_mlstm_out, w_mlstm_out,
               g_kv_norm, w_kv, w_sb_q, w_sb_o,
               g_ffn_norm, w_ffn_up, w_ffn_conv, b_ffn_conv, w_ffn_down, g_final)
    bp = x_prompt.shape[0]
    dt = x_prompt.dtype
    p_c0 = jnp.zeros((N_A_LAYERS, bp, MLSTM_HEADS, MLSTM_QK_DIM, MLSTM_V_DIM), dt)
    p_n0 = jnp.zeros((N_A_LAYERS, bp, MLSTM_HEADS, MLSTM_QK_DIM), dt)
    p_m0 = jnp.full((N_A_LAYERS, bp, MLSTM_HEADS), NEG_BIG, jnp.float32)
    p_conv0 = jnp.zeros((DEPTH, bp, CONV_WIDTH - 1, D_FF), dt)
    p_kv0 = jnp.zeros((bp, 0, SB_HEADS, SB_HEAD_DIM), dt)
    y_prompt, p_mlstm_c, p_mlstm_n, p_mlstm_m, p_ffn_conv, p_k, p_v = trunk(
        x_prompt, p_c0, p_n0, p_m0, p_conv0, p_kv0, p_kv0, *weights)
    y_sample, s_mlstm_c, s_mlstm_n, s_mlstm_m, s_ffn_conv, s_k, s_v = trunk(
        x_sample, state_mlstm_c, state_mlstm_n, state_mlstm_m, state_ffn_conv, cache_k, cache_v, *weights)
    return (y_prompt, y_sample,
            p_mlstm_c, p_mlstm_n, p_mlstm_m, p_ffn_conv, p_k, p_v,
            s_mlstm_c, s_mlstm_n, s_mlstm_m, s_ffn_conv, s_k, s_v)
```

```python
import functools
import math

import jax
import jax.numpy as jnp
from jax import lax
from jax.experimental import pallas as pl
from jax.experimental.pallas import tpu as pltpu

F32 = jnp.float32
BF16 = jnp.bfloat16

D_MODEL = 1024
DEPTH = 4
N_A_LAYERS = 2
MLSTM_HEADS = 8
MLSTM_QK_DIM = 64
MLSTM_V_DIM = 128
MLSTM_PAIRS = MLSTM_HEADS // 2
GATE_SOFTCAP = 15.0
SB_HEADS = 8
SB_HEAD_DIM = 128
D_FF = 2816
NORM_EPS = 1e-6
NEG_BIG = -1e30

LANES = 128
ROW_TILE = 512
FFN_COLS = 256
FFN_CHUNKS = D_FF // FFN_COLS
MLSTM_CHUNK = 128
SB_BLOCK = 256
VMEM_LIMIT = 56 * 1024 * 1024


def _params(n_axes):
    return pltpu.CompilerParams(dimension_semantics=("arbitrary",) * n_axes,
                                vmem_limit_bytes=VMEM_LIMIT)


def _resident(shape):
    zeros = (0,) * len(shape)
    return pl.BlockSpec(shape, lambda *_: zeros, pipeline_mode=pl.Buffered(1))


def _rms(x, g):
    return x * lax.rsqrt(jnp.mean(x * x, axis=-1, keepdims=True) + NORM_EPS) * g


def _dot(a, b):
    return jnp.dot(a, b, preferred_element_type=F32)


def _dot_nt(a, b):
    return lax.dot_general(a, b, (((1,), (1,)), ((), ())), preferred_element_type=F32)


def _split3(x):
    hi = x.astype(BF16)
    r = x - hi.astype(F32)
    mid = r.astype(BF16)
    lo = (r - mid.astype(F32)).astype(BF16)
    return hi, mid, lo


def _log_sigmoid(z):
    return jnp.minimum(z, 0.0) - jnp.log(1.0 + jnp.exp(-jnp.abs(z)))


def _norm_matmul_kernel(x_ref, g_ref, w_ref, *o_refs, col_chunk):
    xn = _rms(x_ref[...], g_ref[...]).astype(BF16)
    n = w_ref.shape[1]
    for c0 in range(0, n, col_chunk):
        y = _dot(xn, w_ref[:, c0:c0 + col_chunk])
        for o_ref in o_refs:
            o_ref[:, c0:c0 + col_chunk] = y.astype(o_ref.dtype)


def _norm_matmul(x, g, w, out_dtypes, name):
    rows, d = x.shape
    n = w.shape[1]
    tm = ROW_TILE
    return pl.pallas_call(
        functools.partial(_norm_matmul_kernel, col_chunk=512),
        grid=(rows // tm,),
        in_specs=[pl.BlockSpec((tm, d), lambda i: (i, 0)), _resident((1, d)), _resident((d, n))],
        out_specs=[pl.BlockSpec((tm, n), lambda i: (i, 0)) for _ in out_dtypes],
        out_shape=[jax.ShapeDtypeStruct((rows, n), dt) for dt in out_dtypes],
        compiler_params=_params(1),
        name=name,
    )(x, g, w)


def _mlstm_in_kernel(x_ref, g_ref, w_ref, wg_ref, bg_ref, o_ref, gate_ref, *, col_chunk):
    xn = _rms(x_ref[...], g_ref[...]).astype(BF16)
    n = w_ref.shape[1]
    for c0 in range(0, n, col_chunk):
        o_ref[:, c0:c0 + col_chunk] = _dot(xn, w_ref[:, c0:c0 + col_chunk]).astype(o_ref.dtype)
    pre = _dot(xn, wg_ref[...]) + bg_ref[...]
    capped = GATE_SOFTCAP * jnp.tanh(pre / GATE_SOFTCAP)
    lane = lax.broadcasted_iota(jnp.int32, capped.shape, 1)
    gate_ref[...] = jnp.where(lane < MLSTM_HEADS, capped, _log_sigmoid(capped))


def _mlstm_in(x, g, w, wg, bg):
    rows, d = x.shape
    n = w.shape[1]
    tm = ROW_TILE
    return pl.pallas_call(
        functools.partial(_mlstm_in_kernel, col_chunk=512),
        grid=(rows // tm,),
        in_specs=[pl.BlockSpec((tm, d), lambda i: (i, 0)), _resident((1, d)), _resident((d, n)),
                  _resident((d, LANES)), _resident((1, LANES))],
        out_specs=[pl.BlockSpec((tm, n), lambda i: (i, 0)),
                   pl.BlockSpec((tm, LANES), lambda i: (i, 0))],
        out_shape=[jax.ShapeDtypeStruct((rows, n), BF16),
                   jax.ShapeDtypeStruct((rows, LANES), F32)],
        compiler_params=_params(1),
        name="mlstm_in_proj",
    )(x, g, w, wg, bg)


def _transpose_rows(a):
    rows = a.shape[0]
    if rows < LANES:
        a = jnp.concatenate([a, jnp.zeros((LANES - rows, LANES), F32)], axis=0)
        return a.T[:, :rows]
    return a.T


def _mlstm_scan_kernel(q_ref, k_ref, v_ref, o_ref, gate_ref, gout_ref, c0_ref, n0_ref, m0_ref,
                       h_ref, c_ref, n_ref, m_ref, c_s, n_s, m_s, *, chunk):
    L = chunk
    j = pl.program_id(1)

    @pl.when(j == 0)
    def _():
        c_s[...] = c0_ref[0]
        n_s[...] = n0_ref[0]
        m_s[...] = m0_ref[0]

    gates = gate_ref[...]
    t_idx = lax.broadcasted_iota(jnp.int32, (L, L), 0)
    s_idx = lax.broadcasted_iota(jnp.int32, (L, L), 1)
    causal = s_idx <= t_idx
    tri = jnp.where(causal, 1.0, 0.0).astype(BF16)
    g_hi, g_mid, g_lo = _split3(gates)
    cum = _dot(tri, g_hi) + _dot(tri, g_mid) + _dot(tri, g_lo)
    gates_t = _transpose_rows(gates)
    cum_t = _transpose_rows(cum)
    lane = lax.broadcasted_iota(jnp.int32, (L, LANES), 1)
    sub = lax.broadcasted_iota(jnp.int32, (LANES, 1), 0)
    m_all = m_s[...]
    m_out = m_all

    for p in range(MLSTM_PAIRS):
        q_pair = q_ref[:, p * LANES:(p + 1) * LANES]
        k_pair = k_ref[:, p * LANES:(p + 1) * LANES]
        q_f32 = q_pair.astype(F32)
        k_f32 = k_pair.astype(F32)
        c_pair = c_s[p]
        c_bf = c_pair.astype(BF16)
        n_pair = n_s[p:p + 1, :]
        kw_parts, v_parts, decays = [], [], []
        for e in range(2):
            h = 2 * p + e
            half = (lane // MLSTM_QK_DIM) == e
            li_col = gates[:, h:h + 1]
            b_col = cum[:, MLSTM_HEADS + h:MLSTM_HEADS + h + 1]
            li_row = gates_t[h:h + 1, :]
            b_row = cum_t[MLSTM_HEADS + h:MLSTM_HEADS + h + 1, :]
            log_d = jnp.where(causal, b_col + (li_row - b_row), NEG_BIG)
            m_prev = m_all[:, h:h + 1]
            m_inter = b_col + m_prev
            m_t = jnp.maximum(m_inter, jnp.max(log_d, axis=-1, keepdims=True))
            w_inter = jnp.exp(m_inter - m_t)
            qm_f32 = jnp.where(half, q_f32, 0.0)
            qm = qm_f32.astype(BF16)
            s = _dot_nt(qm, k_pair) * (MLSTM_QK_DIM ** -0.5) * jnp.exp(log_d - m_t)
            v_h = v_ref[:, h * MLSTM_V_DIM:(h + 1) * MLSTM_V_DIM]
            qc = _dot(qm, c_bf) * (MLSTM_QK_DIM ** -0.5)
            qn = jnp.sum(qm_f32 * n_pair, axis=-1, keepdims=True) * (MLSTM_QK_DIM ** -0.5)
            num = _dot(s.astype(BF16), v_h) + w_inter * qc
            den = jnp.sum(s, axis=-1, keepdims=True) + w_inter * qn
            hh = num / jnp.maximum(jnp.abs(den), jnp.exp(-m_t))
            hh = hh * lax.rsqrt(jnp.mean(hh * hh, axis=-1, keepdims=True) + NORM_EPS)
            gate_o = jax.nn.sigmoid(o_ref[:, h * MLSTM_V_DIM:(h + 1) * MLSTM_V_DIM].astype(F32))
            hh = hh * gout_ref[:, h * MLSTM_V_DIM:(h + 1) * MLSTM_V_DIM] * gate_o
            h_ref[:, h * MLSTM_V_DIM:(h + 1) * MLSTM_V_DIM] = hh.astype(h_ref.dtype)

            m_new = m_t[L - 1:L, :]
            b_last = b_col[L - 1:L, :]
            decays.append(jnp.exp(b_last + m_prev - m_new))
            w_in = jnp.exp(b_last - b_col + li_col - m_new)
            kw_parts.append(jnp.where(half, k_f32 * w_in, 0.0))
            v_parts.append(v_h)
            m_out = jnp.where(lax.broadcasted_iota(jnp.int32, m_out.shape, 1) == h, m_new, m_out)

        kw = jnp.concatenate(kw_parts, axis=0)
        vv = jnp.concatenate(v_parts, axis=0)
        kw_t = kw.T.astype(BF16)
        decay_rows = jnp.where(sub < MLSTM_QK_DIM, decays[0], decays[1])
        decay_lanes = jnp.where(lane[0:1, :] < MLSTM_QK_DIM, decays[0], decays[1])
        c_s[p] = decay_rows * c_pair + _dot(kw_t, vv)
        n_s[p:p + 1, :] = decay_lanes * n_pair + jnp.sum(kw, axis=0, keepdims=True)

    m_s[...] = m_out

    @pl.when(j == pl.num_programs(1) - 1)
    def _():
        c_ref[0] = c_s[...]
        n_ref[0] = n_s[...]
        m_ref[0] = m_s[...]


def _mlstm_scan(proj, gates, g_out, c0, n0, m0, batch, seq):
    L = min(MLSTM_CHUNK, seq)
    nchunk = seq // L
    d = D_MODEL
    qk = MLSTM_HEADS * MLSTM_QK_DIM

    def row(b, j):
        return b * nchunk + j

    return pl.pallas_call(
        functools.partial(_mlstm_scan_kernel, chunk=L),
        grid=(batch, nchunk),
        in_specs=[
            pl.BlockSpec((L, qk), lambda b, j: (row(b, j), 0)),
            pl.BlockSpec((L, qk), lambda b, j: (row(b, j), 1)),
            pl.BlockSpec((L, d), lambda b, j: (row(b, j), 1)),
            pl.BlockSpec((L, d), lambda b, j: (row(b, j), 2)),
            pl.BlockSpec((L, LANES), lambda b, j: (row(b, j), 0)),
            _resident((1, d)),
            pl.BlockSpec((1, MLSTM_PAIRS, LANES, LANES), lambda b, j: (b, 0, 0, 0)),
            pl.BlockSpec((1, MLSTM_PAIRS, LANES), lambda b, j: (b, 0, 0)),
            pl.BlockSpec((1, 1, MLSTM_HEADS), lambda b, j: (b, 0, 0)),
        ],
        out_specs=[
            pl.BlockSpec((L, d), lambda b, j: (row(b, j), 0)),
            pl.BlockSpec((1, MLSTM_PAIRS, LANES, LANES), lambda b, j: (b, 0, 0, 0)),
            pl.BlockSpec((1, MLSTM_PAIRS, LANES), lambda b, j: (b, 0, 0)),
            pl.BlockSpec((1, 1, MLSTM_HEADS), lambda b, j: (b, 0, 0)),
        ],
        out_shape=[
            jax.ShapeDtypeStruct((batch * seq, d), BF16),
            jax.ShapeDtypeStruct((batch, MLSTM_PAIRS, LANES, LANES), F32),
            jax.ShapeDtypeStruct((batch, MLSTM_PAIRS, LANES), F32),
            jax.ShapeDtypeStruct((batch, 1, MLSTM_HEADS), F32),
        ],
        scratch_shapes=[pltpu.VMEM((MLSTM_PAIRS, LANES, LANES), F32),
                        pltpu.VMEM((MLSTM_PAIRS, LANES), F32),
                        pltpu.VMEM((1, MLSTM_HEADS), F32)],
        compiler_params=_params(2),
        name="mlstm_scan",
    )(proj, proj, proj, proj, gates, g_out, c0, n0, m0)


def _proj_residual_kernel(x_ref, y_ref, w_ref, o_ref):
    o_ref[...] = x_ref[...] + _dot(y_ref[...], w_ref[...])


def _proj_residual(x, y, w, name):
    rows, d = x.shape
    k = y.shape[1]
    tm = ROW_TILE
    return pl.pallas_call(
        _proj_residual_kernel,
        grid=(rows // tm,),
        in_specs=[pl.BlockSpec((tm, d), lambda i: (i, 0)), pl.BlockSpec((tm, k), lambda i: (i, 0)),
                  _resident((k, d))],
        out_specs=pl.BlockSpec((tm, d), lambda i: (i, 0)),
        out_shape=jax.ShapeDtypeStruct((rows, d), F32),
        compiler_params=_params(1),
        name=name,
    )(x, y, w)


def _ffn_kernel(x_ref, g_ref, wa_ref, wv_ref, wc_ref, bc_ref, wd_ref, prev_ref,
                o_ref, tail_ref, xn_s, acc_s, carry_s, *, seq, tm):
    sub_len = min(seq, tm)
    n_sub = tm // sub_len
    tiles_per_seq = max(1, seq // tm)
    i = pl.program_id(0)
    starts_seq = (i % tiles_per_seq) == 0

    x = x_ref[...]
    xn_s[...] = _rms(x, g_ref[...]).astype(BF16)
    acc_s[...] = x
    row = lax.broadcasted_iota(jnp.int32, (sub_len, 1), 0)

    if n_sub == 1:
        @pl.when(starts_seq)
        def _():
            carry_s[...] = prev_ref[:, 0]

    def chunk(c, carry):
        xn = xn_s[...]
        a = _dot(xn, wa_ref[c])
        val = _dot(xn, wv_ref[c])
        w = wc_ref[c]
        b = bc_ref[c]
        hs = []
        for s in range(n_sub):
            a_s = a[s * sub_len:(s + 1) * sub_len]
            if n_sub == 1:
                prev = carry_s[c]
            else:
                prev = prev_ref[c, s]
            p0, p1 = prev[0:1], prev[1:2]
            am1 = jnp.where(row == 0, p1, pltpu.roll(a_s, 1, 0))
            am2 = jnp.where(row == 0, p0, jnp.where(row == 1, p1, pltpu.roll(a_s, 2, 0)))
            conv = ((b + w[0:1] * am2) + w[1:2] * am1) + w[2:3] * a_s
            hs.append(jax.nn.gelu(conv) * val[s * sub_len:(s + 1) * sub_len])
            tail = a_s[sub_len - 2:sub_len]
            tail_ref[c, s] = tail
            if n_sub == 1:
                carry_s[c] = tail
        h = hs[0] if n_sub == 1 else jnp.concatenate(hs, axis=0)
        acc_s[...] += _dot(h.astype(BF16), wd_ref[c])
        return carry

    lax.fori_loop(0, FFN_CHUNKS, chunk, 0)
    o_ref[...] = acc_s[...]


def _ffn(x, g, wa, wv, wc, bc, wd, prev, seq):
    rows, d = x.shape
    tm = ROW_TILE
    batch = rows // seq
    if seq >= tm:
        seqs_per_tile = 1
        tiles_per_seq = seq // tm
        seq_block = lambda i: (0, i // tiles_per_seq, 0, 0)
    else:
        seqs_per_tile = tm // seq
        seq_block = lambda i: (0, i, 0, 0)
    state_spec = pl.BlockSpec((FFN_CHUNKS, seqs_per_tile, 2, FFN_COLS), seq_block)
    return pl.pallas_call(
        functools.partial(_ffn_kernel, seq=seq, tm=tm),
        grid=(rows // tm,),
        in_specs=[pl.BlockSpec((tm, d), lambda i: (i, 0)), _resident((1, d)),
                  _resident((FFN_CHUNKS, d, FFN_COLS)), _resident((FFN_CHUNKS, d, FFN_COLS)),
                  _resident((FFN_CHUNKS, 3, FFN_COLS)), _resident((FFN_CHUNKS, 1, FFN_COLS)),
                  _resident((FFN_CHUNKS, FFN_COLS, d)), state_spec],
        out_specs=[pl.BlockSpec((tm, d), lambda i: (i, 0)), state_spec],
        out_shape=[jax.ShapeDtypeStruct((rows, d), F32),
                   jax.ShapeDtypeStruct((FFN_CHUNKS, batch, 2, FFN_COLS), F32)],
        scratch_shapes=[pltpu.VMEM((tm, d), BF16), pltpu.VMEM((tm, d), F32),
                        pltpu.VMEM((FFN_CHUNKS, 2, FFN_COLS), F32)],
        compiler_params=_params(1),
        name="conv_ffn",
    )(x, g, wa, wv, wc, bc, wd, prev)


def _later_matrix(bk):
    j_idx = lax.broadcasted_iota(jnp.int32, (bk, bk), 0)
    s_idx = lax.broadcasted_iota(jnp.int32, (bk, bk), 1)
    return jnp.where(j_idx > s_idx, 1.0, 0.0).astype(BF16)


def _sb_block(q, kb, vb, run, acc, later, strict_lower):
    z = _dot_nt(q, kb) * (SB_HEAD_DIM ** -0.5)
    log_beta = _log_sigmoid(z)
    log_keep = log_beta - z
    if strict_lower is not None:
        log_keep = jnp.where(strict_lower, log_keep, 0.0)
    hi = log_keep.astype(BF16)
    lo = (log_keep - hi.astype(F32)).astype(BF16)
    after = _dot(hi, later) + _dot(lo, later)
    a = jnp.exp(log_beta + after + run)
    if strict_lower is not None:
        a = jnp.where(strict_lower, a, 0.0)
    acc = acc + _dot(a.astype(BF16), vb)
    run = run + after[:, 0:1] + log_keep[:, 0:1]
    return run, acc


def _sb_prompt_kernel(q_ref, k_ref, v_ref, o_ref, *, blk):
    i = pl.program_id(2)
    q = q_ref[...]
    t_idx = lax.broadcasted_iota(jnp.int32, (blk, blk), 0)
    s_idx = lax.broadcasted_iota(jnp.int32, (blk, blk), 1)
    start = pl.multiple_of(i * blk, blk)
    later = _later_matrix(blk)
    run = jnp.zeros((blk, 1), F32)
    acc = jnp.zeros((blk, SB_HEAD_DIM), F32)
    run, acc = _sb_block(q, k_ref[pl.ds(start, blk), :], v_ref[pl.ds(start, blk), :], run, acc,
                         later, s_idx < t_idx)

    def body(n, carry):
        run, acc = carry
        off = pl.multiple_of((i - 1 - n) * blk, blk)
        return _sb_block(q, k_ref[pl.ds(off, blk), :], v_ref[pl.ds(off, blk), :], run, acc,
                         later, None)

    run, acc = lax.fori_loop(0, i, body, (run, acc))
    o_ref[...] = acc.astype(o_ref.dtype)


def _sb_prompt(q, kv, batch, seq):
    blk = SB_BLOCK
    nq = seq // blk
    dh = SB_HEAD_DIM
    return pl.pallas_call(
        functools.partial(_sb_prompt_kernel, blk=blk),
        grid=(batch, SB_HEADS, nq),
        in_specs=[pl.BlockSpec((blk, dh), lambda b, h, i: (b * nq + i, h)),
                  pl.BlockSpec((seq, dh), lambda b, h, i: (b, h)),
                  pl.BlockSpec((seq, dh), lambda b, h, i: (b, SB_HEADS + h))],
        out_specs=pl.BlockSpec((blk, dh), lambda b, h, i: (b * nq + i, h)),
        out_shape=jax.ShapeDtypeStruct((batch * seq, D_MODEL), BF16),
        compiler_params=_params(3),
        name="sb_attention_prompt",
    )(q, kv, kv)


def _sb_cached_kernel(q_ref, k_ref, v_ref, kc_ref, vc_ref, o_ref, *, seq, blk):
    q = q_ref[...]
    t_idx = lax.broadcasted_iota(jnp.int32, (seq, seq), 0)
    s_idx = lax.broadcasted_iota(jnp.int32, (seq, seq), 1)
    run = jnp.zeros((seq, 1), F32)
    acc = jnp.zeros((seq, SB_HEAD_DIM), F32)
    run, acc = _sb_block(q, k_ref[...], v_ref[...], run, acc, _later_matrix(seq), s_idx < t_idx)
    n_blocks = kc_ref.shape[1] // blk
    later = _later_matrix(blk)

    def body(n, carry):
        run, acc = carry
        off = pl.multiple_of((n_blocks - 1 - n) * blk, blk)
        kb = kc_ref[0, pl.ds(off, blk), :].astype(BF16)
        vb = vc_ref[0, pl.ds(off, blk), :].astype(BF16)
        return _sb_block(q, kb, vb, run, acc, later, None)

    run, acc = lax.fori_loop(0, n_blocks, body, (run, acc))
    o_ref[...] = acc.astype(o_ref.dtype)


def _sb_cached(q, kv, cache_k, cache_v, batch, seq):
    dh = SB_HEAD_DIM
    past = cache_k.shape[1]
    return pl.pallas_call(
        functools.partial(_sb_cached_kernel, seq=seq, blk=SB_BLOCK),
        grid=(batch, SB_HEADS),
        in_specs=[pl.BlockSpec((seq, dh), lambda b, h: (b, h)),
                  pl.BlockSpec((seq, dh), lambda b, h: (b, h)),
                  pl.BlockSpec((seq, dh), lambda b, h: (b, SB_HEADS + h)),
                  pl.BlockSpec((1, past, dh), lambda b, h: (b, 0, h)),
                  pl.BlockSpec((1, past, dh), lambda b, h: (b, 0, h))],
        out_specs=pl.BlockSpec((seq, dh), lambda b, h: (b, h)),
        out_shape=jax.ShapeDtypeStruct((batch * seq, D_MODEL), BF16),
        compiler_params=_params(2),
        name="sb_attention_cached",
    )(q, kv, kv, cache_k, cache_v)


def _rmsnorm_kernel(x_ref, g_ref, o_ref):
    o_ref[...] = _rms(x_ref[...], g_ref[...])


def _rmsnorm(x, g):
    rows, d = x.shape
    tm = ROW_TILE
    return pl.pallas_call(
        _rmsnorm_kernel,
        grid=(rows // tm,),
        in_specs=[pl.BlockSpec((tm, d), lambda i: (i, 0)), _resident((1, d))],
        out_specs=pl.BlockSpec((tm, d), lambda i: (i, 0)),
        out_shape=jax.ShapeDtypeStruct((rows, d), F32),
        compiler_params=_params(1),
        name="final_rmsnorm",
    )(x, g)


def _chunk_cols(w):
    lead = w.shape[:-1]
    w = w.reshape(lead + (FFN_CHUNKS, FFN_COLS))
    return jnp.moveaxis(w, -2, 0)


def _prep_weights(g_mix_norm, w_mlstm_in, b_mlstm_gates, g_mlstm_out, w_mlstm_out,
                  g_kv_norm, w_kv, w_sb_q, w_sb_o,
                  g_ffn_norm, w_ffn_up, w_ffn_conv, b_ffn_conv, w_ffn_down, g_final):
    main = 2 * MLSTM_HEADS * MLSTM_QK_DIM + MLSTM_HEADS * MLSTM_V_DIM + D_MODEL
    n_gate = 2 * MLSTM_HEADS
    pad = LANES - n_gate
    return dict(
        g_mix=g_mix_norm.reshape(DEPTH, 1, D_MODEL),
        w_in=w_mlstm_in[:, :, :main].astype(BF16),
        w_gate=jnp.pad(w_mlstm_in[:, :, main:], ((0, 0), (0, 0), (0, pad))).astype(BF16),
        b_gate=jnp.pad(b_mlstm_gates, ((0, 0), (0, pad))).reshape(N_A_LAYERS, 1, LANES),
        g_out=g_mlstm_out.reshape(N_A_LAYERS, 1, D_MODEL),
        w_out=w_mlstm_out.astype(BF16),
        g_kv=g_kv_norm.reshape(1, D_MODEL),
        w_kv=w_kv.astype(BF16),
        w_q=w_sb_q.astype(BF16),
        w_o=w_sb_o.astype(BF16),
        g_ffn=g_ffn_norm.reshape(DEPTH, 1, D_MODEL),
        w_a=jnp.stack([_chunk_cols(w_ffn_up[l, :, :D_FF]) for l in range(DEPTH)]).astype(BF16),
        w_v=jnp.stack([_chunk_cols(w_ffn_up[l, :, D_FF:]) for l in range(DEPTH)]).astype(BF16),
        w_c=jnp.stack([_chunk_cols(w_ffn_conv[l]) for l in range(DEPTH)]),
        b_c=jnp.stack([_chunk_cols(b_ffn_conv[l].reshape(1, D_FF)) for l in range(DEPTH)]),
        w_d=w_ffn_down.reshape(DEPTH, FFN_CHUNKS, FFN_COLS, D_MODEL).astype(BF16),
        g_final=g_final.reshape(1, D_MODEL),
    )


def _trunk(x, c0, n0, m0, conv0, cache, w, batch, seq):
    new_c, new_n, new_m, new_conv = [], [], [], []
    kv_bf = k_new = v_new = None
    for l in range(DEPTH):
        if l < N_A_LAYERS:
            proj, gates = _mlstm_in(x, w["g_mix"][l], w["w_in"][l], w["w_gate"][l], w["b_gate"][l])
            h, c, n, m = _mlstm_scan(
                proj, gates, w["g_out"][l],
                c0[l].reshape(batch, MLSTM_PAIRS, LANES, LANES),
                n0[l].reshape(batch, MLSTM_PAIRS, LANES),
                m0[l].reshape(batch, 1, MLSTM_HEADS), batch, seq)
            x = _proj_residual(x, h, w["w_out"][l], "mlstm_out_proj")
            new_c.append(c.reshape(batch, MLSTM_HEADS, MLSTM_QK_DIM, MLSTM_V_DIM))
            new_n.append(n.reshape(batch, MLSTM_HEADS, MLSTM_QK_DIM))
            new_m.append(m.reshape(batch, MLSTM_HEADS))
        else:
            j = l - N_A_LAYERS
            (q,) = _norm_matmul(x, w["g_mix"][l], w["w_q"][j], [BF16], "sb_q_proj")
            if cache is None:
                att = _sb_prompt(q, kv_bf, batch, seq)
            else:
                att = _sb_cached(q, kv_bf, cache[0], cache[1], batch, seq)
            x = _proj_residual(x, att, w["w_o"][j], "sb_out_proj")
        prev = jnp.moveaxis(conv0[l].reshape(batch, 2, FFN_CHUNKS, FFN_COLS), 2, 0)
        x, tail = _ffn(x, w["g_ffn"][l], w["w_a"][l], w["w_v"][l], w["w_c"][l], w["b_c"][l],
                       w["w_d"][l], prev, seq)
        new_conv.append(jnp.moveaxis(tail, 0, 2).reshape(batch, 2, D_FF))
        if l == N_A_LAYERS - 1:
            kv_f32, kv_bf = _norm_matmul(x, w["g_kv"], w["w_kv"], [F32, BF16], "kv_proj")
            k_new = kv_f32[:, :D_MODEL].reshape(batch, seq, SB_HEADS, SB_HEAD_DIM)
            v_new = kv_f32[:, D_MODEL:].reshape(batch, seq, SB_HEADS, SB_HEAD_DIM)
    y = _rmsnorm(x, w["g_final"]).reshape(batch, seq, D_MODEL)
    return (y, jnp.stack(new_c), jnp.stack(new_n), jnp.stack(new_m), jnp.stack(new_conv),
            k_new, v_new)


def kernel(x_prompt, x_sample, cache_k, cache_v, state_mlstm_c, state_mlstm_n, state_mlstm_m,
           state_ffn_conv, g_mix_norm, w_mlstm_in, b_mlstm_gates, g_mlstm_out, w_mlstm_out,
           g_kv_norm, w_kv, w_sb_q, w_sb_o, g_ffn_norm, w_ffn_up, w_ffn_conv, b_ffn_conv,
           w_ffn_down, g_final):
    w = _prep_weights(g_mix_norm, w_mlstm_in, b_mlstm_gates, g_mlstm_out, w_mlstm_out,
                      g_kv_norm, w_kv, w_sb_q, w_sb_o,
                      g_ffn_norm, w_ffn_up, w_ffn_conv, b_ffn_conv, w_ffn_down, g_final)
    bp, tp, d = x_prompt.shape
    bs, ts, _ = x_sample.shape
    past = cache_k.shape[1]

    p_c0 = jnp.zeros((N_A_LAYERS, bp, MLSTM_HEADS, MLSTM_QK_DIM, MLSTM_V_DIM), F32)
    p_n0 = jnp.zeros((N_A_LAYERS, bp, MLSTM_HEADS, MLSTM_QK_DIM), F32)
    p_m0 = jnp.full((N_A_LAYERS, bp, MLSTM_HEADS), NEG_BIG, F32)
    p_conv0 = jnp.zeros((DEPTH, bp, 2, D_FF), F32)
    p_out = _trunk(x_prompt.reshape(bp * tp, d), p_c0, p_n0, p_m0, p_conv0, None, w, bp, tp)

    cache = (cache_k.reshape(bs, past, d), cache_v.reshape(bs, past, d))
    s_out = _trunk(x_sample.reshape(bs * ts, d), state_mlstm_c, state_mlstm_n, state_mlstm_m,
                   state_ffn_conv, cache, w, bs, ts)
    return (p_out[0], s_out[0]) + p_out[1:] + s_out[1:]
```

```python
import functools
import math

import jax
import jax.numpy as jnp
from jax import lax
from jax.experimental import pallas as pl
from jax.experimental.pallas import tpu as pltpu

F32 = jnp.float32
BF16 = jnp.bfloat16

D_MODEL = 1024
DEPTH = 4
N_A_LAYERS = 2
MLSTM_HEADS = 8
MLSTM_QK_DIM = 64
MLSTM_V_DIM = 128
MLSTM_PAIRS = MLSTM_HEADS // 2
GATE_SOFTCAP = 15.0
SB_HEADS = 8
SB_HEAD_DIM = 128
D_FF = 2816
NORM_EPS = 1e-6
NEG_BIG = -1e30

LANES = 128
ROW_TILE = 512
FFN_COLS = 256
FFN_CHUNKS = D_FF // FFN_COLS
MLSTM_CHUNK = 128
SB_BLOCK = 256
SB_HEADS_PER_STEP = 2
SB_DEAD_LOG = -105.0
VMEM_LIMIT = 56 * 1024 * 1024


def _params(n_axes):
    return pltpu.CompilerParams(dimension_semantics=("arbitrary",) * n_axes,
                                vmem_limit_bytes=VMEM_LIMIT)


def _resident(shape):
    zeros = (0,) * len(shape)
    return pl.BlockSpec(shape, lambda *_: zeros, pipeline_mode=pl.Buffered(1))


def _rms(x, g):
    return x * lax.rsqrt(jnp.mean(x * x, axis=-1, keepdims=True) + NORM_EPS) * g


def _dot(a, b):
    return jnp.dot(a, b, preferred_element_type=F32)


def _dot_nt(a, b):
    return lax.dot_general(a, b, (((1,), (1,)), ((), ())), preferred_element_type=F32)


def _split3(x):
    hi = x.astype(BF16)
    r = x - hi.astype(F32)
    mid = r.astype(BF16)
    lo = (r - mid.astype(F32)).astype(BF16)
    return hi, mid, lo


def _log_sigmoid(z):
    return jnp.minimum(z, 0.0) - jnp.log(1.0 + jnp.exp(-jnp.abs(z)))


def _norm_matmul_kernel(x_ref, g_ref, w_ref, *o_refs, col_chunk):
    xn = _rms(x_ref[...], g_ref[...]).astype(BF16)
    n = w_ref.shape[1]
    for c0 in range(0, n, col_chunk):
        y = _dot(xn, w_ref[:, c0:c0 + col_chunk])
        for o_ref in o_refs:
            o_ref[:, c0:c0 + col_chunk] = y.astype(o_ref.dtype)


def _norm_matmul(x, g, w, out_dtypes, name):
    rows, d = x.shape
    n = w.shape[1]
    tm = ROW_TILE
    return pl.pallas_call(
        functools.partial(_norm_matmul_kernel, col_chunk=512),
        grid=(rows // tm,),
        in_specs=[pl.BlockSpec((tm, d), lambda i: (i, 0)), _resident((1, d)), _resident((d, n))],
        out_specs=[pl.BlockSpec((tm, n), lambda i: (i, 0)) for _ in out_dtypes],
        out_shape=[jax.ShapeDtypeStruct((rows, n), dt) for dt in out_dtypes],
        compiler_params=_params(1),
        name=name,
    )(x, g, w)


def _mlstm_in_kernel(x_ref, g_ref, w_ref, wg_ref, bg_ref, o_ref, gate_ref, *, col_chunk):
    xn = _rms(x_ref[...], g_ref[...]).astype(BF16)
    n = w_ref.shape[1]
    for c0 in range(0, n, col_chunk):
        o_ref[:, c0:c0 + col_chunk] = _dot(xn, w_ref[:, c0:c0 + col_chunk]).astype(o_ref.dtype)
    pre = _dot(xn, wg_ref[...]) + bg_ref[...]
    capped = GATE_SOFTCAP * jnp.tanh(pre / GATE_SOFTCAP)
    lane = lax.broadcasted_iota(jnp.int32, capped.shape, 1)
    gate_ref[...] = jnp.where(lane < MLSTM_HEADS, capped, _log_sigmoid(capped))


def _mlstm_in(x, g, w, wg, bg):
    rows, d = x.shape
    n = w.shape[1]
    tm = ROW_TILE
    return pl.pallas_call(
        functools.partial(_mlstm_in_kernel, col_chunk=512),
        grid=(rows // tm,),
        in_specs=[pl.BlockSpec((tm, d), lambda i: (i, 0)), _resident((1, d)), _resident((d, n)),
                  _resident((d, LANES)), _resident((1, LANES))],
        out_specs=[pl.BlockSpec((tm, n), lambda i: (i, 0)),
                   pl.BlockSpec((tm, LANES), lambda i: (i, 0))],
        out_shape=[jax.ShapeDtypeStruct((rows, n), BF16),
                   jax.ShapeDtypeStruct((rows, LANES), F32)],
        compiler_params=_params(1),
        name="mlstm_in_proj",
    )(x, g, w, wg, bg)


def _transpose_rows(a):
    rows = a.shape[0]
    if rows < LANES:
        a = jnp.concatenate([a, jnp.zeros((LANES - rows, LANES), F32)], axis=0)
        return a.T[:, :rows]
    return a.T


def _mlstm_scan_kernel(q_ref, k_ref, v_ref, o_ref, gate_ref, gout_ref, c0_ref, n0_ref, m0_ref,
                       h_ref, c_ref, n_ref, m_ref, c_s, n_s, m_s, *, chunk):
    L = chunk
    j = pl.program_id(1)

    @pl.when(j == 0)
    def _():
        c_s[...] = c0_ref[0]
        n_s[...] = n0_ref[0]
        m_s[...] = m0_ref[0]

    gates = gate_ref[...]
    t_idx = lax.broadcasted_iota(jnp.int32, (L, L), 0)
    s_idx = lax.broadcasted_iota(jnp.int32, (L, L), 1)
    causal = s_idx <= t_idx
    tri = jnp.where(causal, 1.0, 0.0).astype(BF16)
    g_hi, g_mid, g_lo = _split3(gates)
    cum = _dot(tri, g_hi) + _dot(tri, g_mid) + _dot(tri, g_lo)
    gates_t = _transpose_rows(gates)
    cum_t = _transpose_rows(cum)
    lane = lax.broadcasted_iota(jnp.int32, (L, LANES), 1)
    sub = lax.broadcasted_iota(jnp.int32, (LANES, 1), 0)
    m_all = m_s[...]
    m_out = m_all

    for p in range(MLSTM_PAIRS):
        q_pair = q_ref[:, p * LANES:(p + 1) * LANES]
        k_pair = k_ref[:, p * LANES:(p + 1) * LANES]
        q_f32 = q_pair.astype(F32)
        k_f32 = k_pair.astype(F32)
        c_pair = c_s[p]
        c_bf = c_pair.astype(BF16)
        n_pair = n_s[p:p + 1, :]
        kw_parts, v_parts, decays = [], [], []
        for e in range(2):
            h = 2 * p + e
            half = (lane // MLSTM_QK_DIM) == e
            li_col = gates[:, h:h + 1]
            b_col = cum[:, MLSTM_HEADS + h:MLSTM_HEADS + h + 1]
            li_row = gates_t[h:h + 1, :]
            b_row = cum_t[MLSTM_HEADS + h:MLSTM_HEADS + h + 1, :]
            log_d = jnp.where(causal, b_col + (li_row - b_row), NEG_BIG)
            m_prev = m_all[:, h:h + 1]
            m_inter = b_col + m_prev
            m_t = jnp.maximum(m_inter, jnp.max(log_d, axis=-1, keepdims=True))
            w_inter = jnp.exp(m_inter - m_t)
            qm_f32 = jnp.where(half, q_f32, 0.0)
            qm = qm_f32.astype(BF16)
            s = _dot_nt(qm, k_pair) * (MLSTM_QK_DIM ** -0.5) * jnp.exp(log_d - m_t)
            v_h = v_ref[:, h * MLSTM_V_DIM:(h + 1) * MLSTM_V_DIM]
            qc = _dot(qm, c_bf) * (MLSTM_QK_DIM ** -0.5)
            qn = jnp.sum(qm_f32 * n_pair, axis=-1, keepdims=True) * (MLSTM_QK_DIM ** -0.5)
            num = _dot(s.astype(BF16), v_h) + w_inter * qc
            den = jnp.sum(s, axis=-1, keepdims=True) + w_inter * qn
            hh = num / jnp.maximum(jnp.abs(den), jnp.exp(-m_t))
            hh = hh * lax.rsqrt(jnp.mean(hh * hh, axis=-1, keepdims=True) + NORM_EPS)
            gate_o = jax.nn.sigmoid(o_ref[:, h * MLSTM_V_DIM:(h + 1) * MLSTM_V_DIM].astype(F32))
            hh = hh * gout_ref[:, h * MLSTM_V_DIM:(h + 1) * MLSTM_V_DIM] * gate_o
            h_ref[:, h * MLSTM_V_DIM:(h + 1) * MLSTM_V_DIM] = hh.astype(h_ref.dtype)

            m_new = m_t[L - 1:L, :]
            b_last = b_col[L - 1:L, :]
            decays.append(jnp.exp(b_last + m_prev - m_new))
            w_in = jnp.exp(b_last - b_col + li_col - m_new)
            kw_parts.append(jnp.where(half, k_f32 * w_in, 0.0))
            v_parts.append(v_h)
            m_out = jnp.where(lax.broadcasted_iota(jnp.int32, m_out.shape, 1) == h, m_new, m_out)

        kw = jnp.concatenate(kw_parts, axis=0)
        vv = jnp.concatenate(v_parts, axis=0)
        kw_t = kw.T.astype(BF16)
        decay_rows = jnp.where(sub < MLSTM_QK_DIM, decays[0], decays[1])
        decay_lanes = jnp.where(lane[0:1, :] < MLSTM_QK_DIM, decays[0], decays[1])
        c_s[p] = decay_rows * c_pair + _dot(kw_t, vv)
        n_s[p:p + 1, :] = decay_lanes * n_pair + jnp.sum(kw, axis=0, keepdims=True)

    m_s[...] = m_out

    @pl.when(j == pl.num_programs(1) - 1)
    def _():
        c_ref[0] = c_s[...]
        n_ref[0] = n_s[...]
        m_ref[0] = m_s[...]


def _mlstm_scan(proj, gates, g_out, c0, n0, m0, batch, seq):
    L = min(MLSTM_CHUNK, seq)
    nchunk = seq // L
    d = D_MODEL
    qk = MLSTM_HEADS * MLSTM_QK_DIM

    def row(b, j):
        return b * nchunk + j

    return pl.pallas_call(
        functools.partial(_mlstm_scan_kernel, chunk=L),
        grid=(batch, nchunk),
        in_specs=[
            pl.BlockSpec((L, qk), lambda b, j: (row(b, j), 0)),
            pl.BlockSpec((L, qk), lambda b, j: (row(b, j), 1)),
            pl.BlockSpec((L, d), lambda b, j: (row(b, j), 1)),
            pl.BlockSpec((L, d), lambda b, j: (row(b, j), 2)),
            pl.BlockSpec((L, LANES), lambda b, j: (row(b, j), 0)),
            _resident((1, d)),
            pl.BlockSpec((1, MLSTM_PAIRS, LANES, LANES), lambda b, j: (b, 0, 0, 0)),
            pl.BlockSpec((1, MLSTM_PAIRS, LANES), lambda b, j: (b, 0, 0)),
            pl.BlockSpec((1, 1, MLSTM_HEADS), lambda b, j: (b, 0, 0)),
        ],
        out_specs=[
            pl.BlockSpec((L, d), lambda b, j: (row(b, j), 0)),
            pl.BlockSpec((1, MLSTM_PAIRS, LANES, LANES), lambda b, j: (b, 0, 0, 0)),
            pl.BlockSpec((1, MLSTM_PAIRS, LANES), lambda b, j: (b, 0, 0)),
            pl.BlockSpec((1, 1, MLSTM_HEADS), lambda b, j: (b, 0, 0)),
        ],
        out_shape=[
            jax.ShapeDtypeStruct((batch * seq, d), BF16),
            jax.ShapeDtypeStruct((batch, MLSTM_PAIRS, LANES, LANES), F32),
            jax.ShapeDtypeStruct((batch, MLSTM_PAIRS, LANES), F32),
            jax.ShapeDtypeStruct((batch, 1, MLSTM_HEADS), F32),
        ],
        scratch_shapes=[pltpu.VMEM((MLSTM_PAIRS, LANES, LANES), F32),
                        pltpu.VMEM((MLSTM_PAIRS, LANES), F32),
                        pltpu.VMEM((1, MLSTM_HEADS), F32)],
        compiler_params=_params(2),
        name="mlstm_scan",
    )(proj, proj, proj, proj, gates, g_out, c0, n0, m0)


def _proj_residual_kernel(x_ref, y_ref, w_ref, o_ref):
    o_ref[...] = x_ref[...] + _dot(y_ref[...], w_ref[...])


def _proj_residual(x, y, w, name):
    rows, d = x.shape
    k = y.shape[1]
    tm = ROW_TILE
    return pl.pallas_call(
        _proj_residual_kernel,
        grid=(rows // tm,),
        in_specs=[pl.BlockSpec((tm, d), lambda i: (i, 0)), pl.BlockSpec((tm, k), lambda i: (i, 0)),
                  _resident((k, d))],
        out_specs=pl.BlockSpec((tm, d), lambda i: (i, 0)),
        out_shape=jax.ShapeDtypeStruct((rows, d), F32),
        compiler_params=_params(1),
        name=name,
    )(x, y, w)


def _ffn_kernel(x_ref, g_ref, wa_ref, wv_ref, wc_ref, bc_ref, wd_ref, prev_ref,
                o_ref, tail_ref, xn_s, acc_s, carry_s, *, seq, tm):
    sub_len = min(seq, tm)
    n_sub = tm // sub_len
    tiles_per_seq = max(1, seq // tm)
    i = pl.program_id(0)
    starts_seq = (i % tiles_per_seq) == 0

    x = x_ref[...]
    xn_s[...] = _rms(x, g_ref[...]).astype(BF16)
    acc_s[...] = x
    row = lax.broadcasted_iota(jnp.int32, (sub_len, 1), 0)

    if n_sub == 1:
        @pl.when(starts_seq)
        def _():
            carry_s[...] = prev_ref[:, 0]

    def chunk(c, carry):
        xn = xn_s[...]
        a = _dot(xn, wa_ref[c])
        val = _dot(xn, wv_ref[c])
        w = wc_ref[c]
        b = bc_ref[c]
        hs = []
        for s in range(n_sub):
            a_s = a[s * sub_len:(s + 1) * sub_len]
            if n_sub == 1:
                prev = carry_s[c]
            else:
                prev = prev_ref[c, s]
            p0, p1 = prev[0:1], prev[1:2]
            am1 = jnp.where(row == 0, p1, pltpu.roll(a_s, 1, 0))
            am2 = jnp.where(row == 0, p0, jnp.where(row == 1, p1, pltpu.roll(a_s, 2, 0)))
            conv = ((b + w[0:1] * am2) + w[1:2] * am1) + w[2:3] * a_s
            hs.append(jax.nn.gelu(conv) * val[s * sub_len:(s + 1) * sub_len])
            tail = a_s[sub_len - 2:sub_len]
            tail_ref[c, s] = tail
            if n_sub == 1:
                carry_s[c] = tail
        h = hs[0] if n_sub == 1 else jnp.concatenate(hs, axis=0)
        acc_s[...] += _dot(h.astype(BF16), wd_ref[c])
        return carry

    lax.fori_loop(0, FFN_CHUNKS, chunk, 0)
    o_ref[...] = acc_s[...]


def _ffn(x, g, wa, wv, wc, bc, wd, prev, seq):
    rows, d = x.shape
    tm = ROW_TILE
    batch = rows // seq
    if seq >= tm:
        seqs_per_tile = 1
        tiles_per_seq = seq // tm
        seq_block = lambda i: (0, i // tiles_per_seq, 0, 0)
    else:
        seqs_per_tile = tm // seq
        seq_block = lambda i: (0, i, 0, 0)
    state_spec = pl.BlockSpec((FFN_CHUNKS, seqs_per_tile, 2, FFN_COLS), seq_block)
    return pl.pallas_call(
        functools.partial(_ffn_kernel, seq=seq, tm=tm),
        grid=(rows // tm,),
        in_specs=[pl.BlockSpec((tm, d), lambda i: (i, 0)), _resident((1, d)),
                  _resident((FFN_CHUNKS, d, FFN_COLS)), _resident((FFN_CHUNKS, d, FFN_COLS)),
                  _resident((FFN_CHUNKS, 3, FFN_COLS)), _resident((FFN_CHUNKS, 1, FFN_COLS)),
                  _resident((FFN_CHUNKS, FFN_COLS, d)), state_spec],
        out_specs=[pl.BlockSpec((tm, d), lambda i: (i, 0)), state_spec],
        out_shape=[jax.ShapeDtypeStruct((rows, d), F32),
                   jax.ShapeDtypeStruct((FFN_CHUNKS, batch, 2, FFN_COLS), F32)],
        scratch_shapes=[pltpu.VMEM((tm, d), BF16), pltpu.VMEM((tm, d), F32),
                        pltpu.VMEM((FFN_CHUNKS, 2, FFN_COLS), F32)],
        compiler_params=_params(1),
        name="conv_ffn",
    )(x, g, wa, wv, wc, bc, wd, prev)


def _later_matrix(bk):
    j_idx = lax.broadcasted_iota(jnp.int32, (bk, bk), 0)
    s_idx = lax.broadcasted_iota(jnp.int32, (bk, bk), 1)
    return jnp.where(j_idx > s_idx, 1.0, 0.0).astype(BF16)


def _sb_block(q, kb, vb, run, acc, later, strict_lower):
    z = _dot_nt(q, kb) * (SB_HEAD_DIM ** -0.5)
    log_beta = _log_sigmoid(z)
    log_keep = log_beta - z
    if strict_lower is not None:
        log_keep = jnp.where(strict_lower, log_keep, 0.0)
    hi = log_keep.astype(BF16)
    lo = (log_keep - hi.astype(F32)).astype(BF16)
    after = _dot(hi, later) + _dot(lo, later)
    a = jnp.exp(log_beta + after + run)
    if strict_lower is not None:
        a = jnp.where(strict_lower, a, 0.0)
    acc = acc + _dot(a.astype(BF16), vb)
    run = run + after[:, 0:1] + log_keep[:, 0:1]
    return run, acc


def _sb_alive(state):
    top = state[0][0]
    for run, _ in state[1:]:
        top = jnp.maximum(top, run)
    return jnp.max(top) > SB_DEAD_LOG


def _sb_prompt_kernel(q_ref, k_ref, v_ref, o_ref, *, blk, heads):
    i = pl.program_id(2)
    dh = SB_HEAD_DIM
    cols = [slice(h * dh, (h + 1) * dh) for h in range(heads)]
    qs = [q_ref[:, c] for c in cols]
    t_idx = lax.broadcasted_iota(jnp.int32, (blk, blk), 0)
    s_idx = lax.broadcasted_iota(jnp.int32, (blk, blk), 1)
    later = _later_matrix(blk)

    def visit(off, state, mask):
        return tuple(
            _sb_block(qs[h], k_ref[pl.ds(off, blk), cols[h]], v_ref[pl.ds(off, blk), cols[h]],
                      state[h][0], state[h][1], later, mask)
            for h in range(heads))

    state = tuple((jnp.zeros((blk, 1), F32), jnp.zeros((blk, dh), F32)) for _ in range(heads))
    state = visit(pl.multiple_of(i * blk, blk), state, s_idx < t_idx)

    def cond(carry):
        n, alive, _ = carry
        return jnp.logical_and(n < i, alive)

    def body(carry):
        n, _, state = carry
        state = visit(pl.multiple_of((i - 1 - n) * blk, blk), state, None)
        return n + 1, _sb_alive(state), state

    _, _, state = lax.while_loop(cond, body, (jnp.int32(0), _sb_alive(state), state))
    for h in range(heads):
        o_ref[:, cols[h]] = state[h][1].astype(o_ref.dtype)


def _sb_prompt(q, kv, batch, seq):
    blk = SB_BLOCK
    nq = seq // blk
    heads = SB_HEADS_PER_STEP
    groups = SB_HEADS // heads
    width = heads * SB_HEAD_DIM
    return pl.pallas_call(
        functools.partial(_sb_prompt_kernel, blk=blk, heads=heads),
        grid=(batch, groups, nq),
        in_specs=[pl.BlockSpec((blk, width), lambda b, g, i: (b * nq + i, g)),
                  pl.BlockSpec((seq, width), lambda b, g, i: (b, g)),
                  pl.BlockSpec((seq, width), lambda b, g, i: (b, groups + g))],
        out_specs=pl.BlockSpec((blk, width), lambda b, g, i: (b * nq + i, g)),
        out_shape=jax.ShapeDtypeStruct((batch * seq, D_MODEL), BF16),
        compiler_params=_params(3),
        name="sb_attention_prompt",
    )(q, kv, kv)


def _sb_cached_kernel(q_ref, kv_ref, kc_hbm, vc_hbm, o_ref, kbuf, vbuf, sems, *, seq, blk, n_blocks):
    b = pl.program_id(0)
    dh = SB_HEAD_DIM
    heads = SB_HEADS
    rows = blk * heads
    cols = [slice(h * dh, (h + 1) * dh) for h in range(heads)]
    qs = [q_ref[:, c] for c in cols]

    def copies(n, slot):
        off = pl.multiple_of((n_blocks - 1 - n) * rows, rows)
        return (pltpu.make_async_copy(kc_hbm.at[b, pl.ds(off, rows), :], kbuf.at[slot], sems.at[0, slot]),
                pltpu.make_async_copy(vc_hbm.at[b, pl.ds(off, rows), :], vbuf.at[slot], sems.at[1, slot]))

    def start(n, slot):
        for c in copies(n, slot):
            c.start()

    def wait(n, slot):
        for c in copies(n, slot):
            c.wait()

    start(0, 0)

    t_idx = lax.broadcasted_iota(jnp.int32, (seq, seq), 0)
    s_idx = lax.broadcasted_iota(jnp.int32, (seq, seq), 1)
    later_new = _later_matrix(seq)
    state = tuple(
        _sb_block(qs[h], kv_ref[:, cols[h]], kv_ref[:, D_MODEL + h * dh:D_MODEL + (h + 1) * dh],
                  jnp.zeros((seq, 1), F32), jnp.zeros((seq, dh), F32), later_new, s_idx < t_idx)
        for h in range(heads))
    later = _later_matrix(blk)

    def cond(carry):
        n, alive, _ = carry
        return jnp.logical_and(n < n_blocks, alive)

    def body(carry):
        n, _, state = carry
        slot = n % 2
        wait(n, slot)

        @pl.when(n + 1 < n_blocks)
        def _():
            start(n + 1, 1 - slot)

        state = tuple(
            _sb_block(qs[h],
                      kbuf[slot, pl.ds(h, blk, stride=heads), :].astype(BF16),
                      vbuf[slot, pl.ds(h, blk, stride=heads), :].astype(BF16),
                      state[h][0], state[h][1], later, None)
            for h in range(heads))
        return n + 1, _sb_alive(state), state

    n_done, _, state = lax.while_loop(cond, body, (jnp.int32(0), _sb_alive(state), state))

    @pl.when(n_done < n_blocks)
    def _():
        wait(n_done, n_done % 2)

    for h in range(heads):
        o_ref[:, cols[h]] = state[h][1].astype(o_ref.dtype)


def _sb_cached(q, kv, cache_k, cache_v, batch, seq):
    dh = SB_HEAD_DIM
    blk = SB_BLOCK
    rows = blk * SB_HEADS
    n_blocks = cache_k.shape[1] // rows
    return pl.pallas_call(
        functools.partial(_sb_cached_kernel, seq=seq, blk=blk, n_blocks=n_blocks),
        grid=(batch,),
        in_specs=[pl.BlockSpec((seq, D_MODEL), lambda b: (b, 0)),
                  pl.BlockSpec((seq, 2 * D_MODEL), lambda b: (b, 0)),
                  pl.BlockSpec(memory_space=pl.ANY),
                  pl.BlockSpec(memory_space=pl.ANY)],
        out_specs=pl.BlockSpec((seq, D_MODEL), lambda b: (b, 0)),
        out_shape=jax.ShapeDtypeStruct((batch * seq, D_MODEL), BF16),
        scratch_shapes=[pltpu.VMEM((2, rows, dh), F32), pltpu.VMEM((2, rows, dh), F32),
                        pltpu.SemaphoreType.DMA((2, 2))],
        compiler_params=_params(1),
        name="sb_attention_cached",
    )(q, kv, cache_k, cache_v)


def _rmsnorm_kernel(x_ref, g_ref, o_ref):
    o_ref[...] = _rms(x_ref[...], g_ref[...])


def _rmsnorm(x, g):
    rows, d = x.shape
    tm = ROW_TILE
    return pl.pallas_call(
        _rmsnorm_kernel,
        grid=(rows // tm,),
        in_specs=[pl.BlockSpec((tm, d), lambda i: (i, 0)), _resident((1, d))],
        out_specs=pl.BlockSpec((tm, d), lambda i: (i, 0)),
        out_shape=jax.ShapeDtypeStruct((rows, d), F32),
        compiler_params=_params(1),
        name="final_rmsnorm",
    )(x, g)


def _chunk_cols(w):
    lead = w.shape[:-1]
    w = w.reshape(lead + (FFN_CHUNKS, FFN_COLS))
    return jnp.moveaxis(w, -2, 0)


def _prep_weights(g_mix_norm, w_mlstm_in, b_mlstm_gates, g_mlstm_out, w_mlstm_out,
                  g_kv_norm, w_kv, w_sb_q, w_sb_o,
                  g_ffn_norm, w_ffn_up, w_ffn_conv, b_ffn_conv, w_ffn_down, g_final):
    main = 2 * MLSTM_HEADS * MLSTM_QK_DIM + MLSTM_HEADS * MLSTM_V_DIM + D_MODEL
    n_gate = 2 * MLSTM_HEADS
    pad = LANES - n_gate
    return dict(
        g_mix=g_mix_norm.reshape(DEPTH, 1, D_MODEL),
        w_in=w_mlstm_in[:, :, :main].astype(BF16),
        w_gate=jnp.pad(w_mlstm_in[:, :, main:], ((0, 0), (0, 0), (0, pad))).astype(BF16),
        b_gate=jnp.pad(b_mlstm_gates, ((0, 0), (0, pad))).reshape(N_A_LAYERS, 1, LANES),
        g_out=g_mlstm_out.reshape(N_A_LAYERS, 1, D_MODEL),
        w_out=w_mlstm_out.astype(BF16),
        g_kv=g_kv_norm.reshape(1, D_MODEL),
        w_kv=w_kv.astype(BF16),
        w_q=w_sb_q.astype(BF16),
        w_o=w_sb_o.astype(BF16),
        g_ffn=g_ffn_norm.reshape(DEPTH, 1, D_MODEL),
        w_a=jnp.stack([_chunk_cols(w_ffn_up[l, :, :D_FF]) for l in range(DEPTH)]).astype(BF16),
        w_v=jnp.stack([_chunk_cols(w_ffn_up[l, :, D_FF:]) for l in range(DEPTH)]).astype(BF16),
        w_c=jnp.stack([_chunk_cols(w_ffn_conv[l]) for l in range(DEPTH)]),
        b_c=jnp.stack([_chunk_cols(b_ffn_conv[l].reshape(1, D_FF)) for l in range(DEPTH)]),
        w_d=w_ffn_down.reshape(DEPTH, FFN_CHUNKS, FFN_COLS, D_MODEL).astype(BF16),
        g_final=g_final.reshape(1, D_MODEL),
    )


def _trunk(x, c0, n0, m0, conv0, cache, w, batch, seq):
    new_c, new_n, new_m, new_conv = [], [], [], []
    kv_bf = k_new = v_new = None
    for l in range(DEPTH):
        if l < N_A_LAYERS:
            proj, gates = _mlstm_in(x, w["g_mix"][l], w["w_in"][l], w["w_gate"][l], w["b_gate"][l])
            h, c, n, m = _mlstm_scan(
                proj, gates, w["g_out"][l],
                c0[l].reshape(batch, MLSTM_PAIRS, LANES, LANES),
                n0[l].reshape(batch, MLSTM_PAIRS, LANES),
                m0[l].reshape(batch, 1, MLSTM_HEADS), batch, seq)
            x = _proj_residual(x, h, w["w_out"][l], "mlstm_out_proj")
            new_c.append(c.reshape(batch, MLSTM_HEADS, MLSTM_QK_DIM, MLSTM_V_DIM))
            new_n.append(n.reshape(batch, MLSTM_HEADS, MLSTM_QK_DIM))
            new_m.append(m.reshape(batch, MLSTM_HEADS))
        else:
            j = l - N_A_LAYERS
            (q,) = _norm_matmul(x, w["g_mix"][l], w["w_q"][j], [BF16], "sb_q_proj")
            if cache is None:
                att = _sb_prompt(q, kv_bf, batch, seq)
            else:
                att = _sb_cached(q, kv_bf, cache[0], cache[1], batch, seq)
            x = _proj_residual(x, att, w["w_o"][j], "sb_out_proj")
        prev = jnp.moveaxis(conv0[l].reshape(batch, 2, FFN_CHUNKS, FFN_COLS), 2, 0)
        x, tail = _ffn(x, w["g_ffn"][l], w["w_a"][l], w["w_v"][l], w["w_c"][l], w["b_c"][l],
                       w["w_d"][l], prev, seq)
        new_conv.append(jnp.moveaxis(tail, 0, 2).reshape(batch, 2, D_FF))
        if l == N_A_LAYERS - 1:
            kv_f32, kv_bf = _norm_matmul(x, w["g_kv"], w["w_kv"], [F32, BF16], "kv_proj")
            k_new = kv_f32[:, :D_MODEL].reshape(batch, seq, SB_HEADS, SB_HEAD_DIM)
            v_new = kv_f32[:, D_MODEL:].reshape(batch, seq, SB_HEADS, SB_HEAD_DIM)
    y = _rmsnorm(x, w["g_final"]).reshape(batch, seq, D_MODEL)
    return (y, jnp.stack(new_c), jnp.stack(new_n), jnp.stack(new_m), jnp.stack(new_conv),
            k_new, v_new)


def kernel(x_prompt, x_sample, cache_k, cache_v, state_mlstm_c, state_mlstm_n, state_mlstm_m,
           state_ffn_conv, g_mix_norm, w_mlstm_in, b_mlstm_gates, g_mlstm_out, w_mlstm_out,
           g_kv_norm, w_kv, w_sb_q, w_sb_o, g_ffn_norm, w_ffn_up, w_ffn_conv, b_ffn_conv,
           w_ffn_down, g_final):
    w = _prep_weights(g_mix_norm, w_mlstm_in, b_mlstm_gates, g_mlstm_out, w_mlstm_out,
                      g_kv_norm, w_kv, w_sb_q, w_sb_o,
                      g_ffn_norm, w_ffn_up, w_ffn_conv, b_ffn_conv, w_ffn_down, g_final)
    bp, tp, d = x_prompt.shape
    bs, ts, _ = x_sample.shape
    past = cache_k.shape[1]

    p_c0 = jnp.zeros((N_A_LAYERS, bp, MLSTM_HEADS, MLSTM_QK_DIM, MLSTM_V_DIM), F32)
    p_n0 = jnp.zeros((N_A_LAYERS, bp, MLSTM_HEADS, MLSTM_QK_DIM), F32)
    p_m0 = jnp.full((N_A_LAYERS, bp, MLSTM_HEADS), NEG_BIG, F32)
    p_conv0 = jnp.zeros((DEPTH, bp, 2, D_FF), F32)
    p_out = _trunk(x_prompt.reshape(bp * tp, d), p_c0, p_n0, p_m0, p_conv0, None, w, bp, tp)

    cache = (cache_k.reshape(bs, past * SB_HEADS, SB_HEAD_DIM),
             cache_v.reshape(bs, past * SB_HEADS, SB_HEAD_DIM))
    s_out = _trunk(x_sample.reshape(bs * ts, d), state_mlstm_c, state_mlstm_n, state_mlstm_m,
                   state_ffn_conv, cache, w, bs, ts)
    return (p_out[0], s_out[0]) + p_out[1:] + s_out[1:]
```

```python
import functools

import jax
import jax.numpy as jnp
from jax import lax
from jax.experimental import pallas as pl
from jax.experimental.pallas import tpu as pltpu

F32 = jnp.float32
BF16 = jnp.bfloat16

D_MODEL = 1024
DEPTH = 4
N_A_LAYERS = 2
MLSTM_HEADS = 8
MLSTM_QK_DIM = 64
MLSTM_V_DIM = 128
MLSTM_PAIRS = MLSTM_HEADS // 2
GATE_SOFTCAP = 15.0
SB_HEADS = 8
SB_HEAD_DIM = 128
D_FF = 2816
NORM_EPS = 1e-6
NEG_BIG = -1e30

LANES = 128
ROW_TILE = 512
FFN_COLS = 256
FFN_CHUNKS = D_FF // FFN_COLS
MLSTM_CHUNK = 128
SB_BLOCK = 256
SB_HEADS_PER_STEP = 2
SB_DEAD_LOG = -105.0
VMEM_LIMIT = 56 * 1024 * 1024


def _params(n_axes):
    return pltpu.CompilerParams(dimension_semantics=("arbitrary",) * n_axes,
                                vmem_limit_bytes=VMEM_LIMIT)


def _resident(shape):
    zeros = (0,) * len(shape)
    return pl.BlockSpec(shape, lambda *_: zeros, pipeline_mode=pl.Buffered(1))


def _rms(x, g):
    return x * lax.rsqrt(jnp.mean(x * x, axis=-1, keepdims=True) + NORM_EPS) * g


def _dot(a, b):
    return jnp.dot(a, b, preferred_element_type=F32)


def _dot_nt(a, b):
    return lax.dot_general(a, b, (((1,), (1,)), ((), ())), preferred_element_type=F32)


def _split3(x):
    hi = x.astype(BF16)
    r = x - hi.astype(F32)
    mid = r.astype(BF16)
    lo = (r - mid.astype(F32)).astype(BF16)
    return hi, mid, lo


def _log_sigmoid(z):
    return jnp.minimum(z, 0.0) - jnp.log(1.0 + jnp.exp(-jnp.abs(z)))


def _norm_matmul_kernel(x_ref, g_ref, w_ref, *o_refs, col_chunk):
    xn = _rms(x_ref[...], g_ref[...]).astype(BF16)
    n = w_ref.shape[1]
    for c0 in range(0, n, col_chunk):
        y = _dot(xn, w_ref[:, c0:c0 + col_chunk])
        for o_ref in o_refs:
            o_ref[:, c0:c0 + col_chunk] = y.astype(o_ref.dtype)


def _norm_matmul(x, g, w, out_dtypes, name):
    rows, d = x.shape
    n = w.shape[1]
    tm = ROW_TILE
    return pl.pallas_call(
        functools.partial(_norm_matmul_kernel, col_chunk=512),
        grid=(rows // tm,),
        in_specs=[pl.BlockSpec((tm, d), lambda i: (i, 0)), _resident((1, d)), _resident((d, n))],
        out_specs=[pl.BlockSpec((tm, n), lambda i: (i, 0)) for _ in out_dtypes],
        out_shape=[jax.ShapeDtypeStruct((rows, n), dt) for dt in out_dtypes],
        compiler_params=_params(1),
        name=name,
    )(x, g, w)


def _kv_proj_kernel(x_ref, g_ref, w_ref, k_ref, v_ref, kv_ref):
    xn = _rms(x_ref[...], g_ref[...]).astype(BF16)
    tm = x_ref.shape[0]
    group = 4
    for dst, base in ((k_ref, 0), (v_ref, D_MODEL)):
        for h0 in range(0, SB_HEADS, group):
            c0 = base + h0 * SB_HEAD_DIM
            y = _dot(xn, w_ref[:, c0:c0 + group * SB_HEAD_DIM])
            kv_ref[:, c0:c0 + group * SB_HEAD_DIM] = y.astype(kv_ref.dtype)
            for h in range(group):
                dst[pl.ds(h0 + h, tm, stride=SB_HEADS), :] = y[:, h * SB_HEAD_DIM:(h + 1) * SB_HEAD_DIM]


def _kv_proj(x, g, w):
    rows, d = x.shape
    tm = ROW_TILE
    head_rows = pl.BlockSpec((tm * SB_HEADS, SB_HEAD_DIM), lambda i: (i, 0))
    return pl.pallas_call(
        _kv_proj_kernel,
        grid=(rows // tm,),
        in_specs=[pl.BlockSpec((tm, d), lambda i: (i, 0)), _resident((1, d)), _resident((d, 2 * d))],
        out_specs=[head_rows, head_rows, pl.BlockSpec((tm, 2 * d), lambda i: (i, 0))],
        out_shape=[jax.ShapeDtypeStruct((rows * SB_HEADS, SB_HEAD_DIM), F32),
                   jax.ShapeDtypeStruct((rows * SB_HEADS, SB_HEAD_DIM), F32),
                   jax.ShapeDtypeStruct((rows, 2 * d), BF16)],
        compiler_params=_params(1),
        name="kv_proj",
    )(x, g, w)


def _mlstm_in_kernel(x_ref, g_ref, w_ref, wg_ref, bg_ref, o_ref, gate_ref, *, col_chunk):
    xn = _rms(x_ref[...], g_ref[...]).astype(BF16)
    n = w_ref.shape[1]
    for c0 in range(0, n, col_chunk):
        o_ref[:, c0:c0 + col_chunk] = _dot(xn, w_ref[:, c0:c0 + col_chunk]).astype(o_ref.dtype)
    pre = _dot(xn, wg_ref[...]) + bg_ref[...]
    capped = GATE_SOFTCAP * jnp.tanh(pre / GATE_SOFTCAP)
    lane = lax.broadcasted_iota(jnp.int32, capped.shape, 1)
    gate_ref[...] = jnp.where(lane < MLSTM_HEADS, capped, _log_sigmoid(capped))


def _mlstm_in(x, g, w, wg, bg):
    rows, d = x.shape
    n = w.shape[1]
    tm = ROW_TILE
    return pl.pallas_call(
        functools.partial(_mlstm_in_kernel, col_chunk=512),
        grid=(rows // tm,),
        in_specs=[pl.BlockSpec((tm, d), lambda i: (i, 0)), _resident((1, d)), _resident((d, n)),
                  _resident((d, LANES)), _resident((1, LANES))],
        out_specs=[pl.BlockSpec((tm, n), lambda i: (i, 0)),
                   pl.BlockSpec((tm, LANES), lambda i: (i, 0))],
        out_shape=[jax.ShapeDtypeStruct((rows, n), BF16),
                   jax.ShapeDtypeStruct((rows, LANES), F32)],
        compiler_params=_params(1),
        name="mlstm_in_proj",
    )(x, g, w, wg, bg)


def _transpose_rows(a):
    rows = a.shape[0]
    if rows < LANES:
        a = jnp.concatenate([a, jnp.zeros((LANES - rows, LANES), F32)], axis=0)
        return a.T[:, :rows]
    return a.T


def _mlstm_scan_kernel(q_ref, k_ref, v_ref, o_ref, gate_ref, gout_ref, c0_ref, n0_ref, m0_ref,
                       h_ref, c_ref, n_ref, m_ref, c_s, n_s, m_s, *, chunk):
    L = chunk
    j = pl.program_id(1)

    @pl.when(j == 0)
    def _():
        c_s[...] = c0_ref[0]
        n_s[...] = n0_ref[0]
        m_s[...] = m0_ref[0]

    gates = gate_ref[...]
    t_idx = lax.broadcasted_iota(jnp.int32, (L, L), 0)
    s_idx = lax.broadcasted_iota(jnp.int32, (L, L), 1)
    causal = s_idx <= t_idx
    tri = jnp.where(causal, 1.0, 0.0).astype(BF16)
    g_hi, g_mid, g_lo = _split3(gates)
    cum = _dot(tri, g_hi) + _dot(tri, g_mid) + _dot(tri, g_lo)
    gates_t = _transpose_rows(gates)
    cum_t = _transpose_rows(cum)
    lane = lax.broadcasted_iota(jnp.int32, (L, LANES), 1)
    sub = lax.broadcasted_iota(jnp.int32, (LANES, LANES), 0)
    ones = jnp.ones((2 * L, LANES), BF16)
    scale = MLSTM_QK_DIM ** -0.5

    def row_sum(x):
        hi = x.astype(BF16)
        lo = (x - hi.astype(F32)).astype(BF16)
        width = x.shape[1]
        return _dot(hi, ones[:width]) + _dot(lo, ones[:width])

    for p in range(MLSTM_PAIRS):
        q_pair = q_ref[:, p * LANES:(p + 1) * LANES]
        k_pair = k_ref[:, p * LANES:(p + 1) * LANES]
        q_f32 = q_pair.astype(F32)
        k_f32 = k_pair.astype(F32)
        c_pair = c_s[p]
        n_pair = n_s[p]
        c_bf = c_pair.astype(BF16)
        n_bf = n_pair.astype(BF16)
        kw_parts, v_parts, decays = [], [], []
        for e in range(2):
            h = 2 * p + e
            half = (lane // MLSTM_QK_DIM) == e
            li = jnp.broadcast_to(gates[:, h:h + 1], (L, LANES))
            b = jnp.broadcast_to(cum[:, MLSTM_HEADS + h:MLSTM_HEADS + h + 1], (L, LANES))
            li_row = gates_t[h:h + 1, :]
            b_row = cum_t[MLSTM_HEADS + h:MLSTM_HEADS + h + 1, :]
            log_d = jnp.where(causal, b[:, :L] + (li_row - b_row), NEG_BIG)
            m_prev = m_s[h:h + 1, :]
            m_inter = b + m_prev
            m_t = jnp.maximum(m_inter, jnp.max(log_d, axis=-1, keepdims=True))
            w_inter = jnp.exp(m_inter - m_t)
            qm = jnp.where(half, q_f32, 0.0).astype(BF16)
            s = _dot_nt(qm, k_pair) * scale * jnp.exp(log_d - m_t[:, :L])
            v_h = v_ref[:, h * MLSTM_V_DIM:(h + 1) * MLSTM_V_DIM]
            num = _dot(s.astype(BF16), v_h) + w_inter * (_dot(qm, c_bf) * scale)
            den = row_sum(s) + w_inter * (_dot(qm, n_bf) * scale)
            hh = num / jnp.maximum(jnp.abs(den), jnp.exp(-m_t))
            hh = hh * lax.rsqrt(row_sum(hh * hh) * (1.0 / MLSTM_V_DIM) + NORM_EPS)
            gate_o = jax.nn.sigmoid(o_ref[:, h * MLSTM_V_DIM:(h + 1) * MLSTM_V_DIM].astype(F32))
            hh = hh * gout_ref[:, h * MLSTM_V_DIM:(h + 1) * MLSTM_V_DIM] * gate_o
            h_ref[:, h * MLSTM_V_DIM:(h + 1) * MLSTM_V_DIM] = hh.astype(h_ref.dtype)

            m_new = m_t[L - 1:L, :]
            b_last = b[L - 1:L, :]
            decays.append(jnp.exp(b_last + m_prev - m_new))
            w_in = jnp.exp(b_last - b + li - m_new)
            kw_parts.append(jnp.where(half, k_f32 * w_in, 0.0))
            v_parts.append(v_h)
            m_s[h:h + 1, :] = m_new

        kw = jnp.concatenate(kw_parts, axis=0)
        vv = jnp.concatenate(v_parts, axis=0)
        kw_t = kw.T.astype(BF16)
        decay = jnp.where(sub < MLSTM_QK_DIM, decays[0], decays[1])
        c_s[p] = decay * c_pair + _dot(kw_t, vv)
        n_s[p] = decay * n_pair + _dot(kw_t, ones)

    @pl.when(j == pl.num_programs(1) - 1)
    def _():
        c_ref[0] = c_s[...]
        n_ref[0] = n_s[...]
        m_ref[0] = m_s[...]


def _mlstm_scan(proj, gates, g_out, c0, n0, m0, batch, seq):
    L = min(MLSTM_CHUNK, seq)
    nchunk = seq // L
    d = D_MODEL
    qk = MLSTM_HEADS * MLSTM_QK_DIM

    def row(b, j):
        return b * nchunk + j

    return pl.pallas_call(
        functools.partial(_mlstm_scan_kernel, chunk=L),
        grid=(batch, nchunk),
        in_specs=[
            pl.BlockSpec((L, qk), lambda b, j: (row(b, j), 0)),
            pl.BlockSpec((L, qk), lambda b, j: (row(b, j), 1)),
            pl.BlockSpec((L, d), lambda b, j: (row(b, j), 1)),
            pl.BlockSpec((L, d), lambda b, j: (row(b, j), 2)),
            pl.BlockSpec((L, LANES), lambda b, j: (row(b, j), 0)),
            _resident((1, d)),
            pl.BlockSpec((1, MLSTM_PAIRS, LANES, LANES), lambda b, j: (b, 0, 0, 0)),
            pl.BlockSpec((1, MLSTM_PAIRS, LANES, LANES), lambda b, j: (b, 0, 0, 0)),
            pl.BlockSpec((1, MLSTM_HEADS, LANES), lambda b, j: (b, 0, 0)),
        ],
        out_specs=[
            pl.BlockSpec((L, d), lambda b, j: (row(b, j), 0)),
            pl.BlockSpec((1, MLSTM_PAIRS, LANES, LANES), lambda b, j: (b, 0, 0, 0)),
            pl.BlockSpec((1, MLSTM_PAIRS, LANES, LANES), lambda b, j: (b, 0, 0, 0)),
            pl.BlockSpec((1, MLSTM_HEADS, LANES), lambda b, j: (b, 0, 0)),
        ],
        out_shape=[
            jax.ShapeDtypeStruct((batch * seq, d), BF16),
            jax.ShapeDtypeStruct((batch, MLSTM_PAIRS, LANES, LANES), F32),
            jax.ShapeDtypeStruct((batch, MLSTM_PAIRS, LANES, LANES), F32),
            jax.ShapeDtypeStruct((batch, MLSTM_HEADS, LANES), F32),
        ],
        scratch_shapes=[pltpu.VMEM((MLSTM_PAIRS, LANES, LANES), F32),
                        pltpu.VMEM((MLSTM_PAIRS, LANES, LANES), F32),
                        pltpu.VMEM((MLSTM_HEADS, LANES), F32)],
        compiler_params=_params(2),
        name="mlstm_scan",
    )(proj, proj, proj, proj, gates, g_out, c0, n0, m0)


def _ffn_kernel(x_ref, y_ref, wo_ref, g_ref, wa_ref, wv_ref, wc_ref, bc_ref, wd_ref, prev_ref, gf_ref,
                o_ref, tail_ref, xn_s, acc_s, carry_s, a_s2, v_s2, *, seq, tm, final_norm):
    sub_len = min(seq, tm)
    n_sub = tm // sub_len
    tiles_per_seq = max(1, seq // tm)
    i = pl.program_id(0)
    starts_seq = (i % tiles_per_seq) == 0

    x = x_ref[...] + _dot(y_ref[...], wo_ref[...])
    xn_s[...] = _rms(x, g_ref[...]).astype(BF16)
    acc_s[...] = x
    row = lax.broadcasted_iota(jnp.int32, (sub_len, 1), 0)

    if n_sub == 1:
        @pl.when(starts_seq)
        def _():
            carry_s[...] = prev_ref[:, 0]

    def up(c, slot):
        xn = xn_s[...]
        a_s2[slot] = _dot(xn, wa_ref[c])
        v_s2[slot] = _dot(xn, wv_ref[c])

    def down(c, slot):
        a = a_s2[slot]
        val = v_s2[slot]
        w = wc_ref[c]
        b = bc_ref[c]
        hs = []
        for s in range(n_sub):
            a_s = a[s * sub_len:(s + 1) * sub_len]
            if n_sub == 1:
                prev = carry_s[c]
            else:
                prev = prev_ref[c, s]
            p0, p1 = prev[0:1], prev[1:2]
            am1 = jnp.where(row == 0, p1, pltpu.roll(a_s, 1, 0))
            am2 = jnp.where(row == 0, p0, jnp.where(row == 1, p1, pltpu.roll(a_s, 2, 0)))
            conv = ((b + w[0:1] * am2) + w[1:2] * am1) + w[2:3] * a_s
            hs.append(jax.nn.gelu(conv) * val[s * sub_len:(s + 1) * sub_len])
            tail = a_s[sub_len - 2:sub_len]
            tail_ref[c, s] = tail
            if n_sub == 1:
                carry_s[c] = tail
        h = hs[0] if n_sub == 1 else jnp.concatenate(hs, axis=0)
        acc_s[...] += _dot(h.astype(BF16), wd_ref[c])

    def step(c2, carry):
        c = 2 * c2
        up(c + 1, 1)
        down(c, 0)
        up(c + 2, 0)
        down(c + 1, 1)
        return carry

    up(0, 0)
    lax.fori_loop(0, (FFN_CHUNKS - 1) // 2, step, 0)
    down(FFN_CHUNKS - 1, 0)
    out = acc_s[...]
    o_ref[...] = _rms(out, gf_ref[...]) if final_norm else out


def _ffn(x, y, wo, g, wa, wv, wc, bc, wd, prev, g_final, seq, final_norm):
    assert FFN_CHUNKS % 2 == 1
    rows, d = x.shape
    tm = ROW_TILE
    batch = rows // seq
    if seq >= tm:
        seqs_per_tile = 1
        tiles_per_seq = seq // tm
        seq_block = lambda i: (0, i // tiles_per_seq, 0, 0)
    else:
        seqs_per_tile = tm // seq
        seq_block = lambda i: (0, i, 0, 0)
    state_spec = pl.BlockSpec((FFN_CHUNKS, seqs_per_tile, 2, FFN_COLS), seq_block)
    return pl.pallas_call(
        functools.partial(_ffn_kernel, seq=seq, tm=tm, final_norm=final_norm),
        grid=(rows // tm,),
        in_specs=[pl.BlockSpec((tm, d), lambda i: (i, 0)), pl.BlockSpec((tm, d), lambda i: (i, 0)),
                  _resident((d, d)), _resident((1, d)),
                  _resident((FFN_CHUNKS, d, FFN_COLS)), _resident((FFN_CHUNKS, d, FFN_COLS)),
                  _resident((FFN_CHUNKS, 3, FFN_COLS)), _resident((FFN_CHUNKS, 1, FFN_COLS)),
                  _resident((FFN_CHUNKS, FFN_COLS, d)), state_spec, _resident((1, d))],
        out_specs=[pl.BlockSpec((tm, d), lambda i: (i, 0)), state_spec],
        out_shape=[jax.ShapeDtypeStruct((rows, d), F32),
                   jax.ShapeDtypeStruct((FFN_CHUNKS, batch, 2, FFN_COLS), F32)],
        scratch_shapes=[pltpu.VMEM((tm, d), BF16), pltpu.VMEM((tm, d), F32),
                        pltpu.VMEM((FFN_CHUNKS, 2, FFN_COLS), F32),
                        pltpu.VMEM((2, tm, FFN_COLS), F32), pltpu.VMEM((2, tm, FFN_COLS), F32)],
        compiler_params=_params(1),
        name="conv_ffn",
    )(x, y, wo, g, wa, wv, wc, bc, wd, prev, g_final)


def _later_matrix(bk):
    j_idx = lax.broadcasted_iota(jnp.int32, (bk, bk), 0)
    s_idx = lax.broadcasted_iota(jnp.int32, (bk, bk), 1)
    return jnp.where(j_idx > s_idx, 1.0, 0.0).astype(BF16)


def _sb_block(q, kb, vb, run, acc, later, strict_lower):
    z = _dot_nt(q, kb) * (SB_HEAD_DIM ** -0.5)
    log_beta = _log_sigmoid(z)
    log_keep = log_beta - z
    if strict_lower is not None:
        log_keep = jnp.where(strict_lower, log_keep, 0.0)
    hi = log_keep.astype(BF16)
    lo = (log_keep - hi.astype(F32)).astype(BF16)
    after = _dot(hi, later) + _dot(lo, later)
    a = jnp.exp(log_beta + after + run)
    if strict_lower is not None:
        a = jnp.where(strict_lower, a, 0.0)
    acc = acc + _dot(a.astype(BF16), vb)
    run = run + after[:, 0:1] + log_keep[:, 0:1]
    return run, acc


def _sb_alive(state):
    top = state[0][0]
    for run, _ in state[1:]:
        top = jnp.maximum(top, run)
    return jnp.max(top) > SB_DEAD_LOG


def _sb_prompt_kernel(q_ref, k_ref, v_ref, o_ref, *, blk, heads):
    i = pl.program_id(2)
    dh = SB_HEAD_DIM
    cols = [slice(h * dh, (h + 1) * dh) for h in range(heads)]
    qs = [q_ref[:, c] for c in cols]
    t_idx = lax.broadcasted_iota(jnp.int32, (blk, blk), 0)
    s_idx = lax.broadcasted_iota(jnp.int32, (blk, blk), 1)
    later = _later_matrix(blk)

    def visit(off, state, mask):
        return tuple(
            _sb_block(qs[h], k_ref[pl.ds(off, blk), cols[h]], v_ref[pl.ds(off, blk), cols[h]],
                      state[h][0], state[h][1], later, mask)
            for h in range(heads))

    state = tuple((jnp.zeros((blk, 1), F32), jnp.zeros((blk, dh), F32)) for _ in range(heads))
    state = visit(pl.multiple_of(i * blk, blk), state, s_idx < t_idx)

    def cond(carry):
        n, alive, _ = carry
        return jnp.logical_and(n < i, alive)

    def body(carry):
        n, _, state = carry
        state = visit(pl.multiple_of((i - 1 - n) * blk, blk), state, None)
        return n + 1, _sb_alive(state), state

    _, _, state = lax.while_loop(cond, body, (jnp.int32(0), _sb_alive(state), state))
    for h in range(heads):
        o_ref[:, cols[h]] = state[h][1].astype(o_ref.dtype)


def _sb_prompt(q, kv, batch, seq):
    blk = SB_BLOCK
    nq = seq // blk
    heads = SB_HEADS_PER_STEP
    groups = SB_HEADS // heads
    width = heads * SB_HEAD_DIM
    return pl.pallas_call(
        functools.partial(_sb_prompt_kernel, blk=blk, heads=heads),
        grid=(batch, groups, nq),
        in_specs=[pl.BlockSpec((blk, width), lambda b, g, i: (b * nq + i, g)),
                  pl.BlockSpec((seq, width), lambda b, g, i: (b, g)),
                  pl.BlockSpec((seq, width), lambda b, g, i: (b, groups + g))],
        out_specs=pl.BlockSpec((blk, width), lambda b, g, i: (b * nq + i, g)),
        out_shape=jax.ShapeDtypeStruct((batch * seq, D_MODEL), BF16),
        compiler_params=_params(3),
        name="sb_attention_prompt",
    )(q, kv, kv)


def _sb_cached_kernel(q_ref, kv_ref, kc_hbm, vc_hbm, o_ref, kbuf, vbuf, sems, *, seq, blk, n_blocks):
    b = pl.program_id(0)
    dh = SB_HEAD_DIM
    heads = SB_HEADS
    rows = blk * heads
    cols = [slice(h * dh, (h + 1) * dh) for h in range(heads)]
    qs = [q_ref[:, c] for c in cols]

    def copies(n, slot):
        off = pl.multiple_of((n_blocks - 1 - n) * rows, rows)
        return (pltpu.make_async_copy(kc_hbm.at[b, pl.ds(off, rows), :], kbuf.at[slot], sems.at[0, slot]),
                pltpu.make_async_copy(vc_hbm.at[b, pl.ds(off, rows), :], vbuf.at[slot], sems.at[1, slot]))

    def start(n, slot):
        for c in copies(n, slot):
            c.start()

    def wait(n, slot):
        for c in copies(n, slot):
            c.wait()

    start(0, 0)

    t_idx = lax.broadcasted_iota(jnp.int32, (seq, seq), 0)
    s_idx = lax.broadcasted_iota(jnp.int32, (seq, seq), 1)
    later_new = _later_matrix(seq)
    state = tuple(
        _sb_block(qs[h], kv_ref[:, cols[h]], kv_ref[:, D_MODEL + h * dh:D_MODEL + (h + 1) * dh],
                  jnp.zeros((seq, 1), F32), jnp.zeros((seq, dh), F32), later_new, s_idx < t_idx)
        for h in range(heads))
    later = _later_matrix(blk)

    def cond(carry):
        n, alive, _ = carry
        return jnp.logical_and(n < n_blocks, alive)

    def body(carry):
        n, _, state = carry
        slot = n % 2
        wait(n, slot)

        @pl.when(n + 1 < n_blocks)
        def _():
            start(n + 1, 1 - slot)

        state = tuple(
            _sb_block(qs[h],
                      kbuf[slot, pl.ds(h, blk, stride=heads), :].astype(BF16),
                      vbuf[slot, pl.ds(h, blk, stride=heads), :].astype(BF16),
                      state[h][0], state[h][1], later, None)
            for h in range(heads))
        return n + 1, _sb_alive(state), state

    n_done, _, state = lax.while_loop(cond, body, (jnp.int32(0), _sb_alive(state), state))

    @pl.when(n_done < n_blocks)
    def _():
        wait(n_done, n_done % 2)

    for h in range(heads):
        o_ref[:, cols[h]] = state[h][1].astype(o_ref.dtype)


def _sb_cached(q, kv, cache_k, cache_v, batch, seq):
    dh = SB_HEAD_DIM
    blk = SB_BLOCK
    rows = blk * SB_HEADS
    n_blocks = cache_k.shape[1] // rows
    return pl.pallas_call(
        functools.partial(_sb_cached_kernel, seq=seq, blk=blk, n_blocks=n_blocks),
        grid=(batch,),
        in_specs=[pl.BlockSpec((seq, D_MODEL), lambda b: (b, 0)),
                  pl.BlockSpec((seq, 2 * D_MODEL), lambda b: (b, 0)),
                  pl.BlockSpec(memory_space=pl.ANY),
                  pl.BlockSpec(memory_space=pl.ANY)],
        out_specs=pl.BlockSpec((seq, D_MODEL), lambda b: (b, 0)),
        out_shape=jax.ShapeDtypeStruct((batch * seq, D_MODEL), BF16),
        scratch_shapes=[pltpu.VMEM((2, rows, dh), F32), pltpu.VMEM((2, rows, dh), F32),
                        pltpu.SemaphoreType.DMA((2, 2))],
        compiler_params=_params(1),
        name="sb_attention_cached",
    )(q, kv, cache_k, cache_v)


def _chunk_cols(w):
    lead = w.shape[:-1]
    w = w.reshape(lead + (FFN_CHUNKS, FFN_COLS))
    return jnp.moveaxis(w, -2, 0)


def _prep_weights(g_mix_norm, w_mlstm_in, b_mlstm_gates, g_mlstm_out, w_mlstm_out,
                  g_kv_norm, w_kv, w_sb_q, w_sb_o,
                  g_ffn_norm, w_ffn_up, w_ffn_conv, b_ffn_conv, w_ffn_down, g_final):
    main = 2 * MLSTM_HEADS * MLSTM_QK_DIM + MLSTM_HEADS * MLSTM_V_DIM + D_MODEL
    n_gate = 2 * MLSTM_HEADS
    pad = LANES - n_gate
    return dict(
        g_mix=g_mix_norm.reshape(DEPTH, 1, D_MODEL),
        w_in=w_mlstm_in[:, :, :main].astype(BF16),
        w_gate=jnp.pad(w_mlstm_in[:, :, main:], ((0, 0), (0, 0), (0, pad))).astype(BF16),
        b_gate=jnp.pad(b_mlstm_gates, ((0, 0), (0, pad))).reshape(N_A_LAYERS, 1, LANES),
        g_out=g_mlstm_out.reshape(N_A_LAYERS, 1, D_MODEL),
        w_out=w_mlstm_out.astype(BF16),
        g_kv=g_kv_norm.reshape(1, D_MODEL),
        w_kv=w_kv.astype(BF16),
        w_q=w_sb_q.astype(BF16),
        w_o=w_sb_o.astype(BF16),
        g_ffn=g_ffn_norm.reshape(DEPTH, 1, D_MODEL),
        w_a=jnp.stack([_chunk_cols(w_ffn_up[l, :, :D_FF]) for l in range(DEPTH)]).astype(BF16),
        w_v=jnp.stack([_chunk_cols(w_ffn_up[l, :, D_FF:]) for l in range(DEPTH)]).astype(BF16),
        w_c=jnp.stack([_chunk_cols(w_ffn_conv[l]) for l in range(DEPTH)]),
        b_c=jnp.stack([_chunk_cols(b_ffn_conv[l].reshape(1, D_FF)) for l in range(DEPTH)]),
        w_d=w_ffn_down.reshape(DEPTH, FFN_CHUNKS, FFN_COLS, D_MODEL).astype(BF16),
        g_final=g_final.reshape(1, D_MODEL),
    )


def _trunk(x, c0, n0, m0, conv0, cache, w, batch, seq):
    new_c, new_n, new_m, new_conv = [], [], [], []
    kv_bf = k_new = v_new = None
    for l in range(DEPTH):
        if l < N_A_LAYERS:
            proj, gates = _mlstm_in(x, w["g_mix"][l], w["w_in"][l], w["w_gate"][l], w["b_gate"][l])
            mixed, c, n, m = _mlstm_scan(
                proj, gates, w["g_out"][l],
                c0[l].reshape(batch, MLSTM_PAIRS, LANES, LANES),
                jnp.broadcast_to(n0[l].reshape(batch, MLSTM_PAIRS, LANES, 1),
                                 (batch, MLSTM_PAIRS, LANES, LANES)),
                jnp.broadcast_to(m0[l].reshape(batch, MLSTM_HEADS, 1), (batch, MLSTM_HEADS, LANES)),
                batch, seq)
            w_mix = w["w_out"][l]
            new_c.append(c.reshape(batch, MLSTM_HEADS, MLSTM_QK_DIM, MLSTM_V_DIM))
            new_n.append(n[..., 0].reshape(batch, MLSTM_HEADS, MLSTM_QK_DIM))
            new_m.append(m[..., 0])
        else:
            j = l - N_A_LAYERS
            (q,) = _norm_matmul(x, w["g_mix"][l], w["w_q"][j], [BF16], "sb_q_proj")
            if cache is None:
                mixed = _sb_prompt(q, kv_bf, batch, seq)
            else:
                mixed = _sb_cached(q, kv_bf, cache[0], cache[1], batch, seq)
            w_mix = w["w_o"][j]
        prev = jnp.moveaxis(conv0[l].reshape(batch, 2, FFN_CHUNKS, FFN_COLS), 2, 0)
        x, tail = _ffn(x, mixed, w_mix, w["g_ffn"][l], w["w_a"][l], w["w_v"][l], w["w_c"][l],
                       w["b_c"][l], w["w_d"][l], prev, w["g_final"], seq, l == DEPTH - 1)
        new_conv.append(jnp.moveaxis(tail, 0, 2).reshape(batch, 2, D_FF))
        if l == N_A_LAYERS - 1:
            k_rows, v_rows, kv_bf = _kv_proj(x, w["g_kv"], w["w_kv"])
            k_new = k_rows.reshape(batch, seq, SB_HEADS, SB_HEAD_DIM)
            v_new = v_rows.reshape(batch, seq, SB_HEADS, SB_HEAD_DIM)
    y = x.reshape(batch, seq, D_MODEL)
    return (y, jnp.stack(new_c), jnp.stack(new_n), jnp.stack(new_m), jnp.stack(new_conv),
            k_new, v_new)


def kernel(x_prompt, x_sample, cache_k, cache_v, state_mlstm_c, state_mlstm_n, state_mlstm_m,
           state_ffn_conv, g_mix_norm, w_mlstm_in, b_mlstm_gates, g_mlstm_out, w_mlstm_out,
           g_kv_norm, w_kv, w_sb_q, w_sb_o, g_ffn_norm, w_ffn_up, w_ffn_conv, b_ffn_conv,
           w_ffn_down, g_final):
    w = _prep_weights(g_mix_norm, w_mlstm_in, b_mlstm_gates, g_mlstm_out, w_mlstm_out,
                      g_kv_norm, w_kv, w_sb_q, w_sb_o,
                      g_ffn_norm, w_ffn_up, w_ffn_conv, b_ffn_conv, w_ffn_down, g_final)
    bp, tp, d = x_prompt.shape
    bs, ts, _ = x_sample.shape
    past = cache_k.shape[1]

    p_c0 = jnp.zeros((N_A_LAYERS, bp, MLSTM_HEADS, MLSTM_QK_DIM, MLSTM_V_DIM), F32)
    p_n0 = jnp.zeros((N_A_LAYERS, bp, MLSTM_HEADS, MLSTM_QK_DIM), F32)
    p_m0 = jnp.full((N_A_LAYERS, bp, MLSTM_HEADS), NEG_BIG, F32)
    p_conv0 = jnp.zeros((DEPTH, bp, 2, D_FF), F32)
    p_out = _trunk(x_prompt.reshape(bp * tp, d), p_c0, p_n0, p_m0, p_conv0, None, w, bp, tp)

    cache = (cache_k.reshape(bs, past * SB_HEADS, SB_HEAD_DIM),
             cache_v.reshape(bs, past * SB_HEADS, SB_HEAD_DIM))
    s_out = _trunk(x_sample.reshape(bs * ts, d), state_mlstm_c, state_mlstm_n, state_mlstm_m,
                   state_ffn_conv, cache, w, bs, ts)
    return (p_out[0], s_out[0]) + p_out[1:] + s_out[1:]
```

```python
import functools

import jax
import jax.numpy as jnp
from jax import lax
from jax.experimental import pallas as pl
from jax.experimental.pallas import tpu as pltpu

F32 = jnp.float32
BF16 = jnp.bfloat16

D_MODEL = 1024
DEPTH = 4
N_A_LAYERS = 2
MLSTM_HEADS = 8
MLSTM_QK_DIM = 64
MLSTM_V_DIM = 128
MLSTM_PAIRS = MLSTM_HEADS // 2
GATE_SOFTCAP = 15.0
SB_HEADS = 8
SB_HEAD_DIM = 128
D_FF = 2816
NORM_EPS = 1e-6
NEG_BIG = -1e30

LANES = 128
ROW_TILE = 512
FFN_COLS = 256
FFN_CHUNKS = D_FF // FFN_COLS
MLSTM_CHUNK = 128
SB_BLOCK = 256
SB_HEADS_PER_STEP = 4
SB_DEAD_LOG = -105.0
VMEM_LIMIT = 56 * 1024 * 1024


def _params(n_axes):
    return pltpu.CompilerParams(dimension_semantics=("arbitrary",) * n_axes,
                                vmem_limit_bytes=VMEM_LIMIT)


def _resident(shape):
    zeros = (0,) * len(shape)
    return pl.BlockSpec(shape, lambda *_: zeros, pipeline_mode=pl.Buffered(1))


def _rms(x, g):
    return x * lax.rsqrt(jnp.mean(x * x, axis=-1, keepdims=True) + NORM_EPS) * g


def _dot(a, b):
    return jnp.dot(a, b, preferred_element_type=F32)


def _dot_nt(a, b):
    return lax.dot_general(a, b, (((1,), (1,)), ((), ())), preferred_element_type=F32)


def _split3(x):
    hi = x.astype(BF16)
    r = x - hi.astype(F32)
    mid = r.astype(BF16)
    lo = (r - mid.astype(F32)).astype(BF16)
    return hi, mid, lo


def _log_sigmoid(z):
    return jnp.minimum(z, 0.0) - jnp.log(1.0 + jnp.exp(-jnp.abs(z)))


def _norm_matmul_kernel(x_ref, g_ref, w_ref, *o_refs, col_chunk):
    xn = _rms(x_ref[...], g_ref[...]).astype(BF16)
    n = w_ref.shape[1]
    for c0 in range(0, n, col_chunk):
        y = _dot(xn, w_ref[:, c0:c0 + col_chunk])
        for o_ref in o_refs:
            o_ref[:, c0:c0 + col_chunk] = y.astype(o_ref.dtype)


def _norm_matmul(x, g, w, out_dtypes, name):
    rows, d = x.shape
    n = w.shape[1]
    tm = ROW_TILE
    return pl.pallas_call(
        functools.partial(_norm_matmul_kernel, col_chunk=512),
        grid=(rows // tm,),
        in_specs=[pl.BlockSpec((tm, d), lambda i: (i, 0)), _resident((1, d)), _resident((d, n))],
        out_specs=[pl.BlockSpec((tm, n), lambda i: (i, 0)) for _ in out_dtypes],
        out_shape=[jax.ShapeDtypeStruct((rows, n), dt) for dt in out_dtypes],
        compiler_params=_params(1),
        name=name,
    )(x, g, w)


def _kv_proj_kernel(x_ref, g_ref, w_ref, k_ref, v_ref, kv_ref):
    xn = _rms(x_ref[...], g_ref[...]).astype(BF16)
    tm = x_ref.shape[0]
    group = 4
    for dst, base in ((k_ref, 0), (v_ref, D_MODEL)):
        for h0 in range(0, SB_HEADS, group):
            c0 = base + h0 * SB_HEAD_DIM
            y = _dot(xn, w_ref[:, c0:c0 + group * SB_HEAD_DIM])
            kv_ref[:, c0:c0 + group * SB_HEAD_DIM] = y.astype(kv_ref.dtype)
            for h in range(group):
                dst[pl.ds(h0 + h, tm, stride=SB_HEADS), :] = y[:, h * SB_HEAD_DIM:(h + 1) * SB_HEAD_DIM]


def _kv_proj(x, g, w):
    rows, d = x.shape
    tm = ROW_TILE
    head_rows = pl.BlockSpec((tm * SB_HEADS, SB_HEAD_DIM), lambda i: (i, 0))
    return pl.pallas_call(
        _kv_proj_kernel,
        grid=(rows // tm,),
        in_specs=[pl.BlockSpec((tm, d), lambda i: (i, 0)), _resident((1, d)), _resident((d, 2 * d))],
        out_specs=[head_rows, head_rows, pl.BlockSpec((tm, 2 * d), lambda i: (i, 0))],
        out_shape=[jax.ShapeDtypeStruct((rows * SB_HEADS, SB_HEAD_DIM), F32),
                   jax.ShapeDtypeStruct((rows * SB_HEADS, SB_HEAD_DIM), F32),
                   jax.ShapeDtypeStruct((rows, 2 * d), BF16)],
        compiler_params=_params(1),
        name="kv_proj",
    )(x, g, w)


def _mlstm_in_kernel(x_ref, g_ref, w_ref, wg_ref, bg_ref, o_ref, gate_ref, *, col_chunk):
    xn = _rms(x_ref[...], g_ref[...]).astype(BF16)
    n = w_ref.shape[1]
    for c0 in range(0, n, col_chunk):
        o_ref[:, c0:c0 + col_chunk] = _dot(xn, w_ref[:, c0:c0 + col_chunk]).astype(o_ref.dtype)
    pre = _dot(xn, wg_ref[...]) + bg_ref[...]
    capped = GATE_SOFTCAP * jnp.tanh(pre / GATE_SOFTCAP)
    lane = lax.broadcasted_iota(jnp.int32, capped.shape, 1)
    gate_ref[...] = jnp.where(lane < MLSTM_HEADS, capped, _log_sigmoid(capped))


def _mlstm_in(x, g, w, wg, bg):
    rows, d = x.shape
    n = w.shape[1]
    tm = ROW_TILE
    return pl.pallas_call(
        functools.partial(_mlstm_in_kernel, col_chunk=512),
        grid=(rows // tm,),
        in_specs=[pl.BlockSpec((tm, d), lambda i: (i, 0)), _resident((1, d)), _resident((d, n)),
                  _resident((d, LANES)), _resident((1, LANES))],
        out_specs=[pl.BlockSpec((tm, n), lambda i: (i, 0)),
                   pl.BlockSpec((tm, LANES), lambda i: (i, 0))],
        out_shape=[jax.ShapeDtypeStruct((rows, n), BF16),
                   jax.ShapeDtypeStruct((rows, LANES), F32)],
        compiler_params=_params(1),
        name="mlstm_in_proj",
    )(x, g, w, wg, bg)


def _transpose_rows(a):
    rows = a.shape[0]
    if rows < LANES:
        a = jnp.concatenate([a, jnp.zeros((LANES - rows, LANES), F32)], axis=0)
        return a.T[:, :rows]
    return a.T


def _mlstm_scan_kernel(q_ref, k_ref, v_ref, o_ref, gate_ref, gout_ref, c0_ref, n0_ref, m0_ref,
                       h_ref, c_ref, n_ref, m_ref, c_s, n_s, m_s, *, chunk):
    L = chunk
    j = pl.program_id(1)

    @pl.when(j == 0)
    def _():
        c_s[...] = c0_ref[0]
        n_s[...] = n0_ref[0]
        m_s[...] = m0_ref[0]

    gates = gate_ref[...]
    t_idx = lax.broadcasted_iota(jnp.int32, (L, L), 0)
    s_idx = lax.broadcasted_iota(jnp.int32, (L, L), 1)
    causal = s_idx <= t_idx
    tri = jnp.where(causal, 1.0, 0.0).astype(BF16)
    g_hi, g_mid, g_lo = _split3(gates)
    cum = _dot(tri, g_hi) + _dot(tri, g_mid) + _dot(tri, g_lo)
    gates_t = _transpose_rows(gates)
    cum_t = _transpose_rows(cum)
    lane = lax.broadcasted_iota(jnp.int32, (L, LANES), 1)
    sub = lax.broadcasted_iota(jnp.int32, (LANES, LANES), 0)
    ones = jnp.ones((2 * L, LANES), BF16)
    scale = MLSTM_QK_DIM ** -0.5

    def row_sum(x):
        hi = x.astype(BF16)
        lo = (x - hi.astype(F32)).astype(BF16)
        width = x.shape[1]
        return _dot(hi, ones[:width]) + _dot(lo, ones[:width])

    heads = range(MLSTM_HEADS)
    pairs = range(MLSTM_PAIRS)
    halves = [(lane // MLSTM_QK_DIM) == e for e in range(2)]
    k_pairs = [k_ref[:, p * LANES:(p + 1) * LANES] for p in pairs]
    c_pairs = [c_s[p] for p in pairs]
    n_pairs = [n_s[p] for p in pairs]
    v_heads = [v_ref[:, h * MLSTM_V_DIM:(h + 1) * MLSTM_V_DIM] for h in heads]
    m_prevs = [m_s[h:h + 1, :] for h in heads]

    qk, qc, qn = [], [], []
    for p in pairs:
        q_f32 = q_ref[:, p * LANES:(p + 1) * LANES].astype(F32)
        c_bf = c_pairs[p].astype(BF16)
        n_bf = n_pairs[p].astype(BF16)
        for e in range(2):
            qm = jnp.where(halves[e], q_f32, 0.0).astype(BF16)
            qk.append(_dot_nt(qm, k_pairs[p]))
            qc.append(_dot(qm, c_bf))
            qn.append(_dot(qm, n_bf))

    lis, bs, m_ts, w_inters, ss = [], [], [], [], []
    for h in heads:
        li = jnp.broadcast_to(gates[:, h:h + 1], (L, LANES))
        b = jnp.broadcast_to(cum[:, MLSTM_HEADS + h:MLSTM_HEADS + h + 1], (L, LANES))
        li_row = gates_t[h:h + 1, :]
        b_row = cum_t[MLSTM_HEADS + h:MLSTM_HEADS + h + 1, :]
        log_d = jnp.where(causal, b[:, :L] + (li_row - b_row), NEG_BIG)
        m_inter = b + m_prevs[h]
        m_t = jnp.maximum(m_inter, jnp.max(log_d, axis=-1, keepdims=True))
        lis.append(li)
        bs.append(b)
        m_ts.append(m_t)
        w_inters.append(jnp.exp(m_inter - m_t))
        ss.append(qk[h] * scale * jnp.exp(log_d - m_t[:, :L]))

    sv = [_dot(ss[h].astype(BF16), v_heads[h]) for h in heads]
    s_sum = [row_sum(ss[h]) for h in heads]

    hs = []
    for h in heads:
        num = sv[h] + w_inters[h] * (qc[h] * scale)
        den = s_sum[h] + w_inters[h] * (qn[h] * scale)
        hs.append(num / jnp.maximum(jnp.abs(den), jnp.exp(-m_ts[h])))
    sq_sum = [row_sum(hs[h] * hs[h]) for h in heads]

    kws, decays = [], []
    for h in heads:
        cols = slice(h * MLSTM_V_DIM, (h + 1) * MLSTM_V_DIM)
        hh = hs[h] * lax.rsqrt(sq_sum[h] * (1.0 / MLSTM_V_DIM) + NORM_EPS)
        hh = hh * gout_ref[:, cols] * jax.nn.sigmoid(o_ref[:, cols].astype(F32))
        h_ref[:, cols] = hh.astype(h_ref.dtype)
        m_new = m_ts[h][L - 1:L, :]
        b_last = bs[h][L - 1:L, :]
        decays.append(jnp.exp(b_last + m_prevs[h] - m_new))
        w_in = jnp.exp(b_last - bs[h] + lis[h] - m_new)
        kws.append(jnp.where(halves[h % 2], k_pairs[h // 2].astype(F32) * w_in, 0.0))
        m_s[h:h + 1, :] = m_new

    for p in pairs:
        kw = jnp.concatenate([kws[2 * p], kws[2 * p + 1]], axis=0)
        vv = jnp.concatenate([v_heads[2 * p], v_heads[2 * p + 1]], axis=0)
        kw_t = kw.T.astype(BF16)
        decay = jnp.where(sub < MLSTM_QK_DIM, decays[2 * p], decays[2 * p + 1])
        c_s[p] = decay * c_pairs[p] + _dot(kw_t, vv)
        n_s[p] = decay * n_pairs[p] + _dot(kw_t, ones)

    @pl.when(j == pl.num_programs(1) - 1)
    def _():
        c_ref[0] = c_s[...]
        n_ref[0] = n_s[...]
        m_ref[0] = m_s[...]


def _mlstm_scan(proj, gates, g_out, c0, n0, m0, batch, seq):
    L = min(MLSTM_CHUNK, seq)
    nchunk = seq // L
    d = D_MODEL
    qk = MLSTM_HEADS * MLSTM_QK_DIM

    def row(b, j):
        return b * nchunk + j

    return pl.pallas_call(
        functools.partial(_mlstm_scan_kernel, chunk=L),
        grid=(batch, nchunk),
        in_specs=[
            pl.BlockSpec((L, qk), lambda b, j: (row(b, j), 0)),
            pl.BlockSpec((L, qk), lambda b, j: (row(b, j), 1)),
            pl.BlockSpec((L, d), lambda b, j: (row(b, j), 1)),
            pl.BlockSpec((L, d), lambda b, j: (row(b, j), 2)),
            pl.BlockSpec((L, LANES), lambda b, j: (row(b, j), 0)),
            _resident((1, d)),
            pl.BlockSpec((1, MLSTM_PAIRS, LANES, LANES), lambda b, j: (b, 0, 0, 0)),
            pl.BlockSpec((1, MLSTM_PAIRS, LANES, LANES), lambda b, j: (b, 0, 0, 0)),
            pl.BlockSpec((1, MLSTM_HEADS, LANES), lambda b, j: (b, 0, 0)),
        ],
        out_specs=[
            pl.BlockSpec((L, d), lambda b, j: (row(b, j), 0)),
            pl.BlockSpec((1, MLSTM_PAIRS, LANES, LANES), lambda b, j: (b, 0, 0, 0)),
            pl.BlockSpec((1, MLSTM_PAIRS, LANES, LANES), lambda b, j: (b, 0, 0, 0)),
            pl.BlockSpec((1, MLSTM_HEADS, LANES), lambda b, j: (b, 0, 0)),
        ],
        out_shape=[
            jax.ShapeDtypeStruct((batch * seq, d), BF16),
            jax.ShapeDtypeStruct((batch, MLSTM_PAIRS, LANES, LANES), F32),
            jax.ShapeDtypeStruct((batch, MLSTM_PAIRS, LANES, LANES), F32),
            jax.ShapeDtypeStruct((batch, MLSTM_HEADS, LANES), F32),
        ],
        scratch_shapes=[pltpu.VMEM((MLSTM_PAIRS, LANES, LANES), F32),
                        pltpu.VMEM((MLSTM_PAIRS, LANES, LANES), F32),
                        pltpu.VMEM((MLSTM_HEADS, LANES), F32)],
        compiler_params=_params(2),
        name="mlstm_scan",
    )(proj, proj, proj, proj, gates, g_out, c0, n0, m0)


def _ffn_kernel(x_ref, y_ref, wo_ref, g_ref, wa_ref, wv_ref, wc_ref, bc_ref, wd_ref, prev_ref, gf_ref,
                o_ref, tail_ref, xn_s, acc_s, carry_s, a_s2, v_s2, *, seq, tm, final_norm):
    sub_len = min(seq, tm)
    n_sub = tm // sub_len
    tiles_per_seq = max(1, seq // tm)
    i = pl.program_id(0)
    starts_seq = (i % tiles_per_seq) == 0

    x = x_ref[...] + _dot(y_ref[...], wo_ref[...])
    xn_s[...] = _rms(x, g_ref[...]).astype(BF16)
    acc_s[...] = x
    row = lax.broadcasted_iota(jnp.int32, (sub_len, 1), 0)

    if n_sub == 1:
        @pl.when(starts_seq)
        def _():
            carry_s[...] = prev_ref[:, 0]

    def up(c, slot):
        xn = xn_s[...]
        a_s2[slot] = _dot(xn, wa_ref[c])
        v_s2[slot] = _dot(xn, wv_ref[c])

    def down(c, slot):
        a = a_s2[slot]
        val = v_s2[slot]
        w = wc_ref[c]
        b = bc_ref[c]
        hs = []
        for s in range(n_sub):
            a_s = a[s * sub_len:(s + 1) * sub_len]
            if n_sub == 1:
                prev = carry_s[c]
            else:
                prev = prev_ref[c, s]
            p0, p1 = prev[0:1], prev[1:2]
            am1 = jnp.where(row == 0, p1, pltpu.roll(a_s, 1, 0))
            am2 = jnp.where(row == 0, p0, jnp.where(row == 1, p1, pltpu.roll(a_s, 2, 0)))
            conv = ((b + w[0:1] * am2) + w[1:2] * am1) + w[2:3] * a_s
            hs.append(jax.nn.gelu(conv) * val[s * sub_len:(s + 1) * sub_len])
            tail = a_s[sub_len - 2:sub_len]
            tail_ref[c, s] = tail
            if n_sub == 1:
                carry_s[c] = tail
        h = hs[0] if n_sub == 1 else jnp.concatenate(hs, axis=0)
        acc_s[...] += _dot(h.astype(BF16), wd_ref[c])

    def step(c2, carry):
        c = 2 * c2
        up(c + 1, 1)
        down(c, 0)
        up(c + 2, 0)
        down(c + 1, 1)
        return carry

    up(0, 0)
    lax.fori_loop(0, (FFN_CHUNKS - 1) // 2, step, 0)
    down(FFN_CHUNKS - 1, 0)
    out = acc_s[...]
    o_ref[...] = _rms(out, gf_ref[...]) if final_norm else out


def _ffn(x, y, wo, g, wa, wv, wc, bc, wd, prev, g_final, seq, final_norm):
    assert FFN_CHUNKS % 2 == 1
    rows, d = x.shape
    tm = ROW_TILE
    batch = rows // seq
    if seq >= tm:
        seqs_per_tile = 1
        tiles_per_seq = seq // tm
        seq_block = lambda i: (0, i // tiles_per_seq, 0, 0)
    else:
        seqs_per_tile = tm // seq
        seq_block = lambda i: (0, i, 0, 0)
    state_spec = pl.BlockSpec((FFN_CHUNKS, seqs_per_tile, 2, FFN_COLS), seq_block)
    return pl.pallas_call(
        functools.partial(_ffn_kernel, seq=seq, tm=tm, final_norm=final_norm),
        grid=(rows // tm,),
        in_specs=[pl.BlockSpec((tm, d), lambda i: (i, 0)), pl.BlockSpec((tm, d), lambda i: (i, 0)),
                  _resident((d, d)), _resident((1, d)),
                  _resident((FFN_CHUNKS, d, FFN_COLS)), _resident((FFN_CHUNKS, d, FFN_COLS)),
                  _resident((FFN_CHUNKS, 3, FFN_COLS)), _resident((FFN_CHUNKS, 1, FFN_COLS)),
                  _resident((FFN_CHUNKS, FFN_COLS, d)), state_spec, _resident((1, d))],
        out_specs=[pl.BlockSpec((tm, d), lambda i: (i, 0)), state_spec],
        out_shape=[jax.ShapeDtypeStruct((rows, d), F32),
                   jax.ShapeDtypeStruct((FFN_CHUNKS, batch, 2, FFN_COLS), F32)],
        scratch_shapes=[pltpu.VMEM((tm, d), BF16), pltpu.VMEM((tm, d), F32),
                        pltpu.VMEM((FFN_CHUNKS, 2, FFN_COLS), F32),
                        pltpu.VMEM((2, tm, FFN_COLS), F32), pltpu.VMEM((2, tm, FFN_COLS), F32)],
        compiler_params=_params(1),
        name="conv_ffn",
    )(x, y, wo, g, wa, wv, wc, bc, wd, prev, g_final)


def _later_matrix(bk):
    j_idx = lax.broadcasted_iota(jnp.int32, (bk, bk), 0)
    s_idx = lax.broadcasted_iota(jnp.int32, (bk, bk), 1)
    return jnp.where(j_idx > s_idx, 1.0, 0.0).astype(BF16)


def _sb_visit(qs, kbs, vbs, state, later, strict_lower):
    heads = range(len(qs))
    zs = [_dot_nt(qs[h], kbs[h]) for h in heads]
    log_betas, log_keeps, splits = [], [], []
    for h in heads:
        z = zs[h] * (SB_HEAD_DIM ** -0.5)
        log_beta = _log_sigmoid(z)
        log_keep = log_beta - z
        if strict_lower is not None:
            log_keep = jnp.where(strict_lower, log_keep, 0.0)
        hi = log_keep.astype(BF16)
        log_betas.append(log_beta)
        log_keeps.append(log_keep)
        splits.append((hi, (log_keep - hi.astype(F32)).astype(BF16)))
    afters = [_dot(hi, later) + _dot(lo, later) for hi, lo in splits]
    weights, runs = [], []
    for h in heads:
        a = jnp.exp(log_betas[h] + afters[h] + state[h][0])
        if strict_lower is not None:
            a = jnp.where(strict_lower, a, 0.0)
        weights.append(a.astype(BF16))
        runs.append(state[h][0] + afters[h][:, 0:1] + log_keeps[h][:, 0:1])
    return tuple((runs[h], state[h][1] + _dot(weights[h], vbs[h])) for h in heads)


def _sb_alive(state):
    top = state[0][0]
    for run, _ in state[1:]:
        top = jnp.maximum(top, run)
    return jnp.max(top) > SB_DEAD_LOG


def _sb_prompt_kernel(q_ref, k_ref, v_ref, o_ref, *, blk, heads):
    i = pl.program_id(2)
    dh = SB_HEAD_DIM
    cols = [slice(h * dh, (h + 1) * dh) for h in range(heads)]
    qs = [q_ref[:, c] for c in cols]
    t_idx = lax.broadcasted_iota(jnp.int32, (blk, blk), 0)
    s_idx = lax.broadcasted_iota(jnp.int32, (blk, blk), 1)
    later = _later_matrix(blk)

    def visit(off, state, mask):
        return _sb_visit(qs, [k_ref[pl.ds(off, blk), c] for c in cols],
                         [v_ref[pl.ds(off, blk), c] for c in cols], state, later, mask)

    state = tuple((jnp.zeros((blk, 1), F32), jnp.zeros((blk, dh), F32)) for _ in range(heads))
    state = visit(pl.multiple_of(i * blk, blk), state, s_idx < t_idx)

    def cond(carry):
        n, alive, _ = carry
        return jnp.logical_and(n < i, alive)

    def body(carry):
        n, _, state = carry
        state = visit(pl.multiple_of((i - 1 - n) * blk, blk), state, None)
        return n + 1, _sb_alive(state), state

    _, _, state = lax.while_loop(cond, body, (jnp.int32(0), _sb_alive(state), state))
    for h in range(heads):
        o_ref[:, cols[h]] = state[h][1].astype(o_ref.dtype)


def _sb_prompt(q, kv, batch, seq):
    blk = SB_BLOCK
    nq = seq // blk
    heads = SB_HEADS_PER_STEP
    groups = SB_HEADS // heads
    width = heads * SB_HEAD_DIM
    return pl.pallas_call(
        functools.partial(_sb_prompt_kernel, blk=blk, heads=heads),
        grid=(batch, groups, nq),
        in_specs=[pl.BlockSpec((blk, width), lambda b, g, i: (b * nq + i, g)),
                  pl.BlockSpec((seq, width), lambda b, g, i: (b, g)),
                  pl.BlockSpec((seq, width), lambda b, g, i: (b, groups + g))],
        out_specs=pl.BlockSpec((blk, width), lambda b, g, i: (b * nq + i, g)),
        out_shape=jax.ShapeDtypeStruct((batch * seq, D_MODEL), BF16),
        compiler_params=_params(3),
        name="sb_attention_prompt",
    )(q, kv, kv)


def _sb_cached_kernel(q_ref, kv_ref, kc_hbm, vc_hbm, o_ref, kbuf, vbuf, sems, *, seq, blk, n_blocks):
    b = pl.program_id(0)
    dh = SB_HEAD_DIM
    heads = SB_HEADS
    rows = blk * heads
    cols = [slice(h * dh, (h + 1) * dh) for h in range(heads)]
    qs = [q_ref[:, c] for c in cols]

    def copies(n, slot):
        off = pl.multiple_of((n_blocks - 1 - n) * rows, rows)
        return (pltpu.make_async_copy(kc_hbm.at[b, pl.ds(off, rows), :], kbuf.at[slot], sems.at[0, slot]),
                pltpu.make_async_copy(vc_hbm.at[b, pl.ds(off, rows), :], vbuf.at[slot], sems.at[1, slot]))

    def start(n, slot):
        for c in copies(n, slot):
            c.start()

    def wait(n, slot):
        for c in copies(n, slot):
            c.wait()

    start(0, 0)

    t_idx = lax.broadcasted_iota(jnp.int32, (seq, seq), 0)
    s_idx = lax.broadcasted_iota(jnp.int32, (seq, seq), 1)
    state = tuple((jnp.zeros((seq, 1), F32), jnp.zeros((seq, dh), F32)) for _ in range(heads))
    state = _sb_visit(qs, [kv_ref[:, c] for c in cols],
                      [kv_ref[:, D_MODEL + h * dh:D_MODEL + (h + 1) * dh] for h in range(heads)],
                      state, _later_matrix(seq), s_idx < t_idx)
    later = _later_matrix(blk)

    def cond(carry):
        n, alive, _ = carry
        return jnp.logical_and(n < n_blocks, alive)

    def body(carry):
        n, _, state = carry
        slot = n % 2
        wait(n, slot)

        @pl.when(n + 1 < n_blocks)
        def _():
            start(n + 1, 1 - slot)

        state = _sb_visit(
            qs, [kbuf[slot, pl.ds(h, blk, stride=heads), :].astype(BF16) for h in range(heads)],
            [vbuf[slot, pl.ds(h, blk, stride=heads), :].astype(BF16) for h in range(heads)],
            state, later, None)
        return n + 1, _sb_alive(state), state

    n_done, _, state = lax.while_loop(cond, body, (jnp.int32(0), _sb_alive(state), state))

    @pl.when(n_done < n_blocks)
    def _():
        wait(n_done, n_done % 2)

    for h in range(heads):
        o_ref[:, cols[h]] = state[h][1].astype(o_ref.dtype)


def _sb_cached(q, kv, cache_k, cache_v, batch, seq):
    dh = SB_HEAD_DIM
    blk = SB_BLOCK
    rows = blk * SB_HEADS
    n_blocks = cache_k.shape[1] // rows
    return pl.pallas_call(
        functools.partial(_sb_cached_kernel, seq=seq, blk=blk, n_blocks=n_blocks),
        grid=(batch,),
        in_specs=[pl.BlockSpec((seq, D_MODEL), lambda b: (b, 0)),
                  pl.BlockSpec((seq, 2 * D_MODEL), lambda b: (b, 0)),
                  pl.BlockSpec(memory_space=pl.ANY),
                  pl.BlockSpec(memory_space=pl.ANY)],
        out_specs=pl.BlockSpec((seq, D_MODEL), lambda b: (b, 0)),
        out_shape=jax.ShapeDtypeStruct((batch * seq, D_MODEL), BF16),
        scratch_shapes=[pltpu.VMEM((2, rows, dh), F32), pltpu.VMEM((2, rows, dh), F32),
                        pltpu.SemaphoreType.DMA((2, 2))],
        compiler_params=_params(1),
        name="sb_attention_cached",
    )(q, kv, cache_k, cache_v)


def _chunk_cols(w):
    lead = w.shape[:-1]
    w = w.reshape(lead + (FFN_CHUNKS, FFN_COLS))
    return jnp.moveaxis(w, -2, 0)


def _prep_weights(g_mix_norm, w_mlstm_in, b_mlstm_gates, g_mlstm_out, w_mlstm_out,
                  g_kv_norm, w_kv, w_sb_q, w_sb_o,
                  g_ffn_norm, w_ffn_up, w_ffn_conv, b_ffn_conv, w_ffn_down, g_final):
    main = 2 * MLSTM_HEADS * MLSTM_QK_DIM + MLSTM_HEADS * MLSTM_V_DIM + D_MODEL
    n_gate = 2 * MLSTM_HEADS
    pad = LANES - n_gate
    return dict(
        g_mix=g_mix_norm.reshape(DEPTH, 1, D_MODEL),
        w_in=w_mlstm_in[:, :, :main].astype(BF16),
        w_gate=jnp.pad(w_mlstm_in[:, :, main:], ((0, 0), (0, 0), (0, pad))).astype(BF16),
        b_gate=jnp.pad(b_mlstm_gates, ((0, 0), (0, pad))).reshape(N_A_LAYERS, 1, LANES),
        g_out=g_mlstm_out.reshape(N_A_LAYERS, 1, D_MODEL),
        w_out=w_mlstm_out.astype(BF16),
        g_kv=g_kv_norm.reshape(1, D_MODEL),
        w_kv=w_kv.astype(BF16),
        w_q=w_sb_q.astype(BF16),
        w_o=w_sb_o.astype(BF16),
        g_ffn=g_ffn_norm.reshape(DEPTH, 1, D_MODEL),
        w_a=jnp.stack([_chunk_cols(w_ffn_up[l, :, :D_FF]) for l in range(DEPTH)]).astype(BF16),
        w_v=jnp.stack([_chunk_cols(w_ffn_up[l, :, D_FF:]) for l in range(DEPTH)]).astype(BF16),
        w_c=jnp.stack([_chunk_cols(w_ffn_conv[l]) for l in range(DEPTH)]),
        b_c=jnp.stack([_chunk_cols(b_ffn_conv[l].reshape(1, D_FF)) for l in range(DEPTH)]),
        w_d=w_ffn_down.reshape(DEPTH, FFN_CHUNKS, FFN_COLS, D_MODEL).astype(BF16),
        g_final=g_final.reshape(1, D_MODEL),
    )


def _trunk(x, c0, n0, m0, conv0, cache, w, batch, seq):
    new_c, new_n, new_m, new_conv = [], [], [], []
    kv_bf = k_new = v_new = None
    for l in range(DEPTH):
        if l < N_A_LAYERS:
            proj, gates = _mlstm_in(x, w["g_mix"][l], w["w_in"][l], w["w_gate"][l], w["b_gate"][l])
            mixed, c, n, m = _mlstm_scan(
                proj, gates, w["g_out"][l],
                c0[l].reshape(batch, MLSTM_PAIRS, LANES, LANES),
                jnp.broadcast_to(n0[l].reshape(batch, MLSTM_PAIRS, LANES, 1),
                                 (batch, MLSTM_PAIRS, LANES, LANES)),
                jnp.broadcast_to(m0[l].reshape(batch, MLSTM_HEADS, 1), (batch, MLSTM_HEADS, LANES)),
                batch, seq)
            w_mix = w["w_out"][l]
            new_c.append(c.reshape(batch, MLSTM_HEADS, MLSTM_QK_DIM, MLSTM_V_DIM))
            new_n.append(n[..., 0].reshape(batch, MLSTM_HEADS, MLSTM_QK_DIM))
            new_m.append(m[..., 0])
        else:
            j = l - N_A_LAYERS
            (q,) = _norm_matmul(x, w["g_mix"][l], w["w_q"][j], [BF16], "sb_q_proj")
            if cache is None:
                mixed = _sb_prompt(q, kv_bf, batch, seq)
            else:
                mixed = _sb_cached(q, kv_bf, cache[0], cache[1], batch, seq)
            w_mix = w["w_o"][j]
        prev = jnp.moveaxis(conv0[l].reshape(batch, 2, FFN_CHUNKS, FFN_COLS), 2, 0)
        x, tail = _ffn(x, mixed, w_mix, w["g_ffn"][l], w["w_a"][l], w["w_v"][l], w["w_c"][l],
                       w["b_c"][l], w["w_d"][l], prev, w["g_final"], seq, l == DEPTH - 1)
        new_conv.append(jnp.moveaxis(tail, 0, 2).reshape(batch, 2, D_FF))
        if l == N_A_LAYERS - 1:
            k_rows, v_rows, kv_bf = _kv_proj(x, w["g_kv"], w["w_kv"])
            k_new = k_rows.reshape(batch, seq, SB_HEADS, SB_HEAD_DIM)
            v_new = v_rows.reshape(batch, seq, SB_HEADS, SB_HEAD_DIM)
    y = x.reshape(batch, seq, D_MODEL)
    return (y, jnp.stack(new_c), jnp.stack(new_n), jnp.stack(new_m), jnp.stack(new_conv),
            k_new, v_new)


def kernel(x_prompt, x_sample, cache_k, cache_v, state_mlstm_c, state_mlstm_n, state_mlstm_m,
           state_ffn_conv, g_mix_norm, w_mlstm_in, b_mlstm_gates, g_mlstm_out, w_mlstm_out,
           g_kv_norm, w_kv, w_sb_q, w_sb_o, g_ffn_norm, w_ffn_up, w_ffn_conv, b_ffn_conv,
           w_ffn_down, g_final):
    w = _prep_weights(g_mix_norm, w_mlstm_in, b_mlstm_gates, g_mlstm_out, w_mlstm_out,
                      g_kv_norm, w_kv, w_sb_q, w_sb_o,
                      g_ffn_norm, w_ffn_up, w_ffn_conv, b_ffn_conv, w_ffn_down, g_final)
    bp, tp, d = x_prompt.shape
    bs, ts, _ = x_sample.shape
    past = cache_k.shape[1]

    p_c0 = jnp.zeros((N_A_LAYERS, bp, MLSTM_HEADS, MLSTM_QK_DIM, MLSTM_V_DIM), F32)
    p_n0 = jnp.zeros((N_A_LAYERS, bp, MLSTM_HEADS, MLSTM_QK_DIM), F32)
    p_m0 = jnp.full((N_A_LAYERS, bp, MLSTM_HEADS), NEG_BIG, F32)
    p_conv0 = jnp.zeros((DEPTH, bp, 2, D_FF), F32)
    p_out = _trunk(x_prompt.reshape(bp * tp, d), p_c0, p_n0, p_m0, p_conv0, None, w, bp, tp)

    cache = (cache_k.reshape(bs, past * SB_HEADS, SB_HEAD_DIM),
             cache_v.reshape(bs, past * SB_HEADS, SB_HEAD_DIM))
    s_out = _trunk(x_sample.reshape(bs * ts, d), state_mlstm_c, state_mlstm_n, state_mlstm_m,
                   state_ffn_conv, cache, w, bs, ts)
    return (p_out[0], s_out[0]) + p_out[1:] + s_out[1:]
```

```python
import functools

import jax
import jax.numpy as jnp
from jax import lax
from jax.experimental import pallas as pl
from jax.experimental.pallas import tpu as pltpu

F32 = jnp.float32
BF16 = jnp.bfloat16

D_MODEL = 1024
DEPTH = 4
N_A_LAYERS = 2
MLSTM_HEADS = 8
MLSTM_QK_DIM = 64
MLSTM_V_DIM = 128
MLSTM_PAIRS = MLSTM_HEADS // 2
GATE_SOFTCAP = 15.0
SB_HEADS = 8
SB_HEAD_DIM = 128
D_FF = 2816
NORM_EPS = 1e-6
NEG_BIG = -1e30

LANES = 128
ROW_TILE = 512
FFN_COLS = 256
FFN_CHUNKS = D_FF // FFN_COLS
MLSTM_CHUNK = 128
MLSTM_CHUNKS_PER_STEP = 2
SB_BLOCK = 256
SB_HEADS_PER_STEP = 4
SB_DEAD_LOG = -105.0
VMEM_LIMIT = 56 * 1024 * 1024


def _params(n_axes):
    return pltpu.CompilerParams(dimension_semantics=("arbitrary",) * n_axes,
                                vmem_limit_bytes=VMEM_LIMIT)


def _resident(shape):
    zeros = (0,) * len(shape)
    return pl.BlockSpec(shape, lambda *_: zeros, pipeline_mode=pl.Buffered(1))


def _rms(x, g):
    return x * lax.rsqrt(jnp.mean(x * x, axis=-1, keepdims=True) + NORM_EPS) * g


def _dot(a, b):
    return jnp.dot(a, b, preferred_element_type=F32)


def _dot_nt(a, b):
    return lax.dot_general(a, b, (((1,), (1,)), ((), ())), preferred_element_type=F32)


def _split3(x):
    hi = x.astype(BF16)
    r = x - hi.astype(F32)
    mid = r.astype(BF16)
    lo = (r - mid.astype(F32)).astype(BF16)
    return hi, mid, lo


def _log_sigmoid(z):
    return jnp.minimum(z, 0.0) - jnp.log(1.0 + jnp.exp(-jnp.abs(z)))


def _norm_matmul_kernel(x_ref, g_ref, w_ref, *o_refs, col_chunk):
    xn = _rms(x_ref[...], g_ref[...]).astype(BF16)
    n = w_ref.shape[1]
    for c0 in range(0, n, col_chunk):
        y = _dot(xn, w_ref[:, c0:c0 + col_chunk])
        for o_ref in o_refs:
            o_ref[:, c0:c0 + col_chunk] = y.astype(o_ref.dtype)


def _norm_matmul(x, g, w, out_dtypes, name):
    rows, d = x.shape
    n = w.shape[1]
    tm = ROW_TILE
    return pl.pallas_call(
        functools.partial(_norm_matmul_kernel, col_chunk=512),
        grid=(rows // tm,),
        in_specs=[pl.BlockSpec((tm, d), lambda i: (i, 0)), _resident((1, d)), _resident((d, n))],
        out_specs=[pl.BlockSpec((tm, n), lambda i: (i, 0)) for _ in out_dtypes],
        out_shape=[jax.ShapeDtypeStruct((rows, n), dt) for dt in out_dtypes],
        compiler_params=_params(1),
        name=name,
    )(x, g, w)


def _kv_proj_kernel(x_ref, g_ref, w_ref, k_ref, v_ref, kv_ref):
    xn = _rms(x_ref[...], g_ref[...]).astype(BF16)
    tm = x_ref.shape[0]
    group = 4
    for dst, base in ((k_ref, 0), (v_ref, D_MODEL)):
        for h0 in range(0, SB_HEADS, group):
            c0 = base + h0 * SB_HEAD_DIM
            y = _dot(xn, w_ref[:, c0:c0 + group * SB_HEAD_DIM])
            kv_ref[:, c0:c0 + group * SB_HEAD_DIM] = y.astype(kv_ref.dtype)
            for h in range(group):
                dst[pl.ds(h0 + h, tm, stride=SB_HEADS), :] = y[:, h * SB_HEAD_DIM:(h + 1) * SB_HEAD_DIM]


def _kv_proj(x, g, w):
    rows, d = x.shape
    tm = ROW_TILE
    head_rows = pl.BlockSpec((tm * SB_HEADS, SB_HEAD_DIM), lambda i: (i, 0))
    return pl.pallas_call(
        _kv_proj_kernel,
        grid=(rows // tm,),
        in_specs=[pl.BlockSpec((tm, d), lambda i: (i, 0)), _resident((1, d)), _resident((d, 2 * d))],
        out_specs=[head_rows, head_rows, pl.BlockSpec((tm, 2 * d), lambda i: (i, 0))],
        out_shape=[jax.ShapeDtypeStruct((rows * SB_HEADS, SB_HEAD_DIM), F32),
                   jax.ShapeDtypeStruct((rows * SB_HEADS, SB_HEAD_DIM), F32),
                   jax.ShapeDtypeStruct((rows, 2 * d), BF16)],
        compiler_params=_params(1),
        name="kv_proj",
    )(x, g, w)


def _mlstm_in_kernel(x_ref, g_ref, w_ref, wg_ref, bg_ref, o_ref, gate_ref, *, col_chunk):
    xn = _rms(x_ref[...], g_ref[...]).astype(BF16)
    n = w_ref.shape[1]
    for c0 in range(0, n, col_chunk):
        o_ref[:, c0:c0 + col_chunk] = _dot(xn, w_ref[:, c0:c0 + col_chunk]).astype(o_ref.dtype)
    pre = _dot(xn, wg_ref[...]) + bg_ref[...]
    capped = GATE_SOFTCAP * jnp.tanh(pre / GATE_SOFTCAP)
    lane = lax.broadcasted_iota(jnp.int32, capped.shape, 1)
    gate_ref[...] = jnp.where(lane < MLSTM_HEADS, capped, _log_sigmoid(capped))


def _mlstm_in(x, g, w, wg, bg):
    rows, d = x.shape
    n = w.shape[1]
    tm = ROW_TILE
    return pl.pallas_call(
        functools.partial(_mlstm_in_kernel, col_chunk=512),
        grid=(rows // tm,),
        in_specs=[pl.BlockSpec((tm, d), lambda i: (i, 0)), _resident((1, d)), _resident((d, n)),
                  _resident((d, LANES)), _resident((1, LANES))],
        out_specs=[pl.BlockSpec((tm, n), lambda i: (i, 0)),
                   pl.BlockSpec((tm, LANES), lambda i: (i, 0))],
        out_shape=[jax.ShapeDtypeStruct((rows, n), BF16),
                   jax.ShapeDtypeStruct((rows, LANES), F32)],
        compiler_params=_params(1),
        name="mlstm_in_proj",
    )(x, g, w, wg, bg)


def _transpose_rows(a):
    rows = a.shape[0]
    if rows < LANES:
        a = jnp.concatenate([a, jnp.zeros((LANES - rows, LANES), F32)], axis=0)
        return a.T[:, :rows]
    return a.T


def _mlstm_chunk(rows, q_ref, k_ref, v_ref, o_ref, gate_ref, gout_ref, h_ref, c_s, n_s, m_s):
    L = rows.stop - rows.start
    gates = gate_ref[rows, :]
    t_idx = lax.broadcasted_iota(jnp.int32, (L, L), 0)
    s_idx = lax.broadcasted_iota(jnp.int32, (L, L), 1)
    causal = s_idx <= t_idx
    tri = jnp.where(causal, 1.0, 0.0).astype(BF16)
    g_hi, g_mid, g_lo = _split3(gates)
    cum = _dot(tri, g_hi) + _dot(tri, g_mid) + _dot(tri, g_lo)
    gates_t = _transpose_rows(gates)
    cum_t = _transpose_rows(cum)
    lane = lax.broadcasted_iota(jnp.int32, (L, LANES), 1)
    sub = lax.broadcasted_iota(jnp.int32, (LANES, LANES), 0)
    ones = jnp.ones((2 * L, LANES), BF16)
    scale = MLSTM_QK_DIM ** -0.5

    def row_sum(x):
        hi = x.astype(BF16)
        lo = (x - hi.astype(F32)).astype(BF16)
        width = x.shape[1]
        return _dot(hi, ones[:width]) + _dot(lo, ones[:width])

    heads = range(MLSTM_HEADS)
    pairs = range(MLSTM_PAIRS)
    halves = [(lane // MLSTM_QK_DIM) == e for e in range(2)]
    k_pairs = [k_ref[rows, p * LANES:(p + 1) * LANES] for p in pairs]
    c_pairs = [c_s[p] for p in pairs]
    n_pairs = [n_s[p] for p in pairs]
    v_heads = [v_ref[rows, h * MLSTM_V_DIM:(h + 1) * MLSTM_V_DIM] for h in heads]
    m_prevs = [m_s[h:h + 1, :] for h in heads]

    qk, qc, qn = [], [], []
    for p in pairs:
        q_f32 = q_ref[rows, p * LANES:(p + 1) * LANES].astype(F32)
        c_bf = c_pairs[p].astype(BF16)
        n_bf = n_pairs[p].astype(BF16)
        for e in range(2):
            qm = jnp.where(halves[e], q_f32, 0.0).astype(BF16)
            qk.append(_dot_nt(qm, k_pairs[p]))
            qc.append(_dot(qm, c_bf))
            qn.append(_dot(qm, n_bf))

    lis, bs, m_ts, w_inters, ss = [], [], [], [], []
    for h in heads:
        li = jnp.broadcast_to(gates[:, h:h + 1], (L, LANES))
        b = jnp.broadcast_to(cum[:, MLSTM_HEADS + h:MLSTM_HEADS + h + 1], (L, LANES))
        li_row = gates_t[h:h + 1, :]
        b_row = cum_t[MLSTM_HEADS + h:MLSTM_HEADS + h + 1, :]
        log_d = jnp.where(causal, b[:, :L] + (li_row - b_row), NEG_BIG)
        m_inter = b + m_prevs[h]
        m_t = jnp.maximum(m_inter, jnp.max(log_d, axis=-1, keepdims=True))
        lis.append(li)
        bs.append(b)
        m_ts.append(m_t)
        w_inters.append(jnp.exp(m_inter - m_t))
        ss.append(qk[h] * scale * jnp.exp(log_d - m_t[:, :L]))

    sv = [_dot(ss[h].astype(BF16), v_heads[h]) for h in heads]
    s_sum = [row_sum(ss[h]) for h in heads]

    hs = []
    for h in heads:
        num = sv[h] + w_inters[h] * (qc[h] * scale)
        den = s_sum[h] + w_inters[h] * (qn[h] * scale)
        hs.append(num / jnp.maximum(jnp.abs(den), jnp.exp(-m_ts[h])))
    sq_sum = [row_sum(hs[h] * hs[h]) for h in heads]

    kws, decays = [], []
    for h in heads:
        cols = slice(h * MLSTM_V_DIM, (h + 1) * MLSTM_V_DIM)
        hh = hs[h] * lax.rsqrt(sq_sum[h] * (1.0 / MLSTM_V_DIM) + NORM_EPS)
        hh = hh * gout_ref[:, cols] * jax.nn.sigmoid(o_ref[rows, cols].astype(F32))
        h_ref[rows, cols] = hh.astype(h_ref.dtype)
        m_new = m_ts[h][L - 1:L, :]
        b_last = bs[h][L - 1:L, :]
        decays.append(jnp.exp(b_last + m_prevs[h] - m_new))
        w_in = jnp.exp(b_last - bs[h] + lis[h] - m_new)
        kws.append(jnp.where(halves[h % 2], k_pairs[h // 2].astype(F32) * w_in, 0.0))
        m_s[h:h + 1, :] = m_new

    for p in pairs:
        kw = jnp.concatenate([kws[2 * p], kws[2 * p + 1]], axis=0)
        vv = jnp.concatenate([v_heads[2 * p], v_heads[2 * p + 1]], axis=0)
        kw_t = kw.T.astype(BF16)
        decay = jnp.where(sub < MLSTM_QK_DIM, decays[2 * p], decays[2 * p + 1])
        c_s[p] = decay * c_pairs[p] + _dot(kw_t, vv)
        n_s[p] = decay * n_pairs[p] + _dot(kw_t, ones)


def _mlstm_scan_kernel(q_ref, k_ref, v_ref, o_ref, gate_ref, gout_ref, c0_ref, n0_ref, m0_ref,
                       h_ref, c_ref, n_ref, m_ref, c_s, n_s, m_s, *, chunk, chunks):
    j = pl.program_id(1)

    @pl.when(j == 0)
    def _():
        c_s[...] = c0_ref[0]
        n_s[...] = n0_ref[0]
        m_s[...] = m0_ref[0]

    for ci in range(chunks):
        _mlstm_chunk(slice(ci * chunk, (ci + 1) * chunk), q_ref, k_ref, v_ref, o_ref, gate_ref,
                     gout_ref, h_ref, c_s, n_s, m_s)

    @pl.when(j == pl.num_programs(1) - 1)
    def _():
        c_ref[0] = c_s[...]
        n_ref[0] = n_s[...]
        m_ref[0] = m_s[...]


def _mlstm_scan(proj, gates, g_out, c0, n0, m0, batch, seq):
    L = min(MLSTM_CHUNK, seq)
    chunks = min(MLSTM_CHUNKS_PER_STEP, seq // L)
    R = L * chunks
    nchunk = seq // R
    d = D_MODEL
    qk = MLSTM_HEADS * MLSTM_QK_DIM

    def row(b, j):
        return b * nchunk + j

    return pl.pallas_call(
        functools.partial(_mlstm_scan_kernel, chunk=L, chunks=chunks),
        grid=(batch, nchunk),
        in_specs=[
            pl.BlockSpec((R, qk), lambda b, j: (row(b, j), 0)),
            pl.BlockSpec((R, qk), lambda b, j: (row(b, j), 1)),
            pl.BlockSpec((R, d), lambda b, j: (row(b, j), 1)),
            pl.BlockSpec((R, d), lambda b, j: (row(b, j), 2)),
            pl.BlockSpec((R, LANES), lambda b, j: (row(b, j), 0)),
            _resident((1, d)),
            pl.BlockSpec((1, MLSTM_PAIRS, LANES, LANES), lambda b, j: (b, 0, 0, 0)),
            pl.BlockSpec((1, MLSTM_PAIRS, LANES, LANES), lambda b, j: (b, 0, 0, 0)),
            pl.BlockSpec((1, MLSTM_HEADS, LANES), lambda b, j: (b, 0, 0)),
        ],
        out_specs=[
            pl.BlockSpec((R, d), lambda b, j: (row(b, j), 0)),
            pl.BlockSpec((1, MLSTM_PAIRS, LANES, LANES), lambda b, j: (b, 0, 0, 0)),
            pl.BlockSpec((1, MLSTM_PAIRS, LANES, LANES), lambda b, j: (b, 0, 0, 0)),
            pl.BlockSpec((1, MLSTM_HEADS, LANES), lambda b, j: (b, 0, 0)),
        ],
        out_shape=[
            jax.ShapeDtypeStruct((batch * seq, d), BF16),
            jax.ShapeDtypeStruct((batch, MLSTM_PAIRS, LANES, LANES), F32),
            jax.ShapeDtypeStruct((batch, MLSTM_PAIRS, LANES, LANES), F32),
            jax.ShapeDtypeStruct((batch, MLSTM_HEADS, LANES), F32),
        ],
        scratch_shapes=[pltpu.VMEM((MLSTM_PAIRS, LANES, LANES), F32),
                        pltpu.VMEM((MLSTM_PAIRS, LANES, LANES), F32),
                        pltpu.VMEM((MLSTM_HEADS, LANES), F32)],
        compiler_params=_params(2),
        name="mlstm_scan",
    )(proj, proj, proj, proj, gates, g_out, c0, n0, m0)


def _ffn_kernel(x_ref, y_ref, wo_ref, g_ref, wa_ref, wv_ref, wc_ref, bc_ref, wd_ref, prev_ref, gf_ref,
                o_ref, tail_ref, xn_s, acc_s, carry_s, a_s2, v_s2, *, seq, tm, final_norm):
    sub_len = min(seq, tm)
    n_sub = tm // sub_len
    tiles_per_seq = max(1, seq // tm)
    i = pl.program_id(0)
    starts_seq = (i % tiles_per_seq) == 0

    x = x_ref[...] + _dot(y_ref[...], wo_ref[...])
    xn_s[...] = _rms(x, g_ref[...]).astype(BF16)
    acc_s[...] = x
    row = lax.broadcasted_iota(jnp.int32, (sub_len, 1), 0)

    if n_sub == 1:
        @pl.when(starts_seq)
        def _():
            carry_s[...] = prev_ref[:, 0]

    def up(c, slot):
        xn = xn_s[...]
        a_s2[slot] = _dot(xn, wa_ref[c])
        v_s2[slot] = _dot(xn, wv_ref[c])

    def down(c, slot):
        a = a_s2[slot]
        val = v_s2[slot]
        w = wc_ref[c]
        b = bc_ref[c]
        hs = []
        for s in range(n_sub):
            a_s = a[s * sub_len:(s + 1) * sub_len]
            if n_sub == 1:
                prev = carry_s[c]
            else:
                prev = prev_ref[c, s]
            p0, p1 = prev[0:1], prev[1:2]
            am1 = jnp.where(row == 0, p1, pltpu.roll(a_s, 1, 0))
            am2 = jnp.where(row == 0, p0, jnp.where(row == 1, p1, pltpu.roll(a_s, 2, 0)))
            conv = ((b + w[0:1] * am2) + w[1:2] * am1) + w[2:3] * a_s
            hs.append(jax.nn.gelu(conv) * val[s * sub_len:(s + 1) * sub_len])
            tail = a_s[sub_len - 2:sub_len]
            tail_ref[c, s] = tail
            if n_sub == 1:
                carry_s[c] = tail
        h = hs[0] if n_sub == 1 else jnp.concatenate(hs, axis=0)
        acc_s[...] += _dot(h.astype(BF16), wd_ref[c])

    def step(c2, carry):
        c = 2 * c2
        up(c + 1, 1)
        down(c, 0)
        up(c + 2, 0)
        down(c + 1, 1)
        return carry

    up(0, 0)
    lax.fori_loop(0, (FFN_CHUNKS - 1) // 2, step, 0)
    down(FFN_CHUNKS - 1, 0)
    out = acc_s[...]
    o_ref[...] = _rms(out, gf_ref[...]) if final_norm else out


def _ffn(x, y, wo, g, w_up, wc, bc, w_dn, prev, g_final, layer, seq, final_norm):
    assert FFN_CHUNKS % 2 == 1
    rows, d = x.shape

    def up_branch(branch):
        return pl.BlockSpec((None, None, FFN_CHUNKS, d, FFN_COLS),
                            lambda i: (layer, branch, 0, 0, 0), pipeline_mode=pl.Buffered(1))

    down_spec = pl.BlockSpec((None, FFN_CHUNKS, FFN_COLS, d), lambda i: (layer, 0, 0, 0),
                             pipeline_mode=pl.Buffered(1))
    tm = ROW_TILE
    batch = rows // seq
    if seq >= tm:
        seqs_per_tile = 1
        tiles_per_seq = seq // tm
        seq_block = lambda i: (0, i // tiles_per_seq, 0, 0)
    else:
        seqs_per_tile = tm // seq
        seq_block = lambda i: (0, i, 0, 0)
    state_spec = pl.BlockSpec((FFN_CHUNKS, seqs_per_tile, 2, FFN_COLS), seq_block)
    return pl.pallas_call(
        functools.partial(_ffn_kernel, seq=seq, tm=tm, final_norm=final_norm),
        grid=(rows // tm,),
        in_specs=[pl.BlockSpec((tm, d), lambda i: (i, 0)), pl.BlockSpec((tm, d), lambda i: (i, 0)),
                  _resident((d, d)), _resident((1, d)), up_branch(0), up_branch(1),
                  _resident((FFN_CHUNKS, 3, FFN_COLS)), _resident((FFN_CHUNKS, 1, FFN_COLS)),
                  down_spec, state_spec, _resident((1, d))],
        out_specs=[pl.BlockSpec((tm, d), lambda i: (i, 0)), state_spec],
        out_shape=[jax.ShapeDtypeStruct((rows, d), F32),
                   jax.ShapeDtypeStruct((FFN_CHUNKS, batch, 2, FFN_COLS), F32)],
        scratch_shapes=[pltpu.VMEM((tm, d), BF16), pltpu.VMEM((tm, d), F32),
                        pltpu.VMEM((FFN_CHUNKS, 2, FFN_COLS), F32),
                        pltpu.VMEM((2, tm, FFN_COLS), F32), pltpu.VMEM((2, tm, FFN_COLS), F32)],
        compiler_params=_params(1),
        name="conv_ffn",
    )(x, y, wo, g, w_up, w_up, wc, bc, w_dn, prev, g_final)


def _later_matrix(bk):
    j_idx = lax.broadcasted_iota(jnp.int32, (bk, bk), 0)
    s_idx = lax.broadcasted_iota(jnp.int32, (bk, bk), 1)
    return jnp.where(j_idx > s_idx, 1.0, 0.0).astype(BF16)


def _sb_visit(qs, kbs, vbs, state, later, strict_lower):
    heads = range(len(qs))
    zs = [_dot_nt(qs[h], kbs[h]) for h in heads]
    log_betas, log_keeps, splits = [], [], []
    for h in heads:
        z = zs[h] * (SB_HEAD_DIM ** -0.5)
        log_beta = _log_sigmoid(z)
        log_keep = log_beta - z
        if strict_lower is not None:
            log_keep = jnp.where(strict_lower, log_keep, 0.0)
        hi = log_keep.astype(BF16)
        log_betas.append(log_beta)
        log_keeps.append(log_keep)
        splits.append((hi, (log_keep - hi.astype(F32)).astype(BF16)))
    afters = [_dot(hi, later) + _dot(lo, later) for hi, lo in splits]
    weights, runs = [], []
    for h in heads:
        a = jnp.exp(log_betas[h] + afters[h] + state[h][0])
        if strict_lower is not None:
            a = jnp.where(strict_lower, a, 0.0)
        weights.append(a.astype(BF16))
        runs.append(state[h][0] + afters[h][:, 0:1] + log_keeps[h][:, 0:1])
    return tuple((runs[h], state[h][1] + _dot(weights[h], vbs[h])) for h in heads)


def _sb_alive(state):
    top = state[0][0]
    for run, _ in state[1:]:
        top = jnp.maximum(top, run)
    return jnp.max(top) > SB_DEAD_LOG


def _sb_prompt_kernel(q_ref, k_ref, v_ref, o_ref, *, blk, heads):
    i = pl.program_id(2)
    dh = SB_HEAD_DIM
    cols = [slice(h * dh, (h + 1) * dh) for h in range(heads)]
    qs = [q_ref[:, c] for c in cols]
    t_idx = lax.broadcasted_iota(jnp.int32, (blk, blk), 0)
    s_idx = lax.broadcasted_iota(jnp.int32, (blk, blk), 1)
    later = _later_matrix(blk)

    def visit(off, state, mask):
        return _sb_visit(qs, [k_ref[pl.ds(off, blk), c] for c in cols],
                         [v_ref[pl.ds(off, blk), c] for c in cols], state, later, mask)

    state = tuple((jnp.zeros((blk, 1), F32), jnp.zeros((blk, dh), F32)) for _ in range(heads))
    state = visit(pl.multiple_of(i * blk, blk), state, s_idx < t_idx)

    def cond(carry):
        n, alive, _ = carry
        return jnp.logical_and(n < i, alive)

    def body(carry):
        n, _, state = carry
        state = visit(pl.multiple_of((i - 1 - n) * blk, blk), state, None)
        return n + 1, _sb_alive(state), state

    _, _, state = lax.while_loop(cond, body, (jnp.int32(0), _sb_alive(state), state))
    for h in range(heads):
        o_ref[:, cols[h]] = state[h][1].astype(o_ref.dtype)


def _sb_prompt(q, kv, batch, seq):
    blk = SB_BLOCK
    nq = seq // blk
    heads = SB_HEADS_PER_STEP
    groups = SB_HEADS // heads
    width = heads * SB_HEAD_DIM
    return pl.pallas_call(
        functools.partial(_sb_prompt_kernel, blk=blk, heads=heads),
        grid=(batch, groups, nq),
        in_specs=[pl.BlockSpec((blk, width), lambda b, g, i: (b * nq + i, g)),
                  pl.BlockSpec((seq, width), lambda b, g, i: (b, g)),
                  pl.BlockSpec((seq, width), lambda b, g, i: (b, groups + g))],
        out_specs=pl.BlockSpec((blk, width), lambda b, g, i: (b * nq + i, g)),
        out_shape=jax.ShapeDtypeStruct((batch * seq, D_MODEL), BF16),
        compiler_params=_params(3),
        name="sb_attention_prompt",
    )(q, kv, kv)


def _sb_cached_kernel(q_ref, kv_ref, kc_hbm, vc_hbm, o_ref, kbuf, vbuf, sems, *, seq, blk, n_blocks):
    b = pl.program_id(0)
    dh = SB_HEAD_DIM
    heads = SB_HEADS
    rows = blk * heads
    cols = [slice(h * dh, (h + 1) * dh) for h in range(heads)]
    qs = [q_ref[:, c] for c in cols]

    def copies(n, slot):
        off = pl.multiple_of((n_blocks - 1 - n) * rows, rows)
        return (pltpu.make_async_copy(kc_hbm.at[b, pl.ds(off, rows), :], kbuf.at[slot], sems.at[0, slot]),
                pltpu.make_async_copy(vc_hbm.at[b, pl.ds(off, rows), :], vbuf.at[slot], sems.at[1, slot]))

    def start(n, slot):
        for c in copies(n, slot):
            c.start()

    def wait(n, slot):
        for c in copies(n, slot):
            c.wait()

    start(0, 0)

    t_idx = lax.broadcasted_iota(jnp.int32, (seq, seq), 0)
    s_idx = lax.broadcasted_iota(jnp.int32, (seq, seq), 1)
    state = tuple((jnp.zeros((seq, 1), F32), jnp.zeros((seq, dh), F32)) for _ in range(heads))
    state = _sb_visit(qs, [kv_ref[:, c] for c in cols],
                      [kv_ref[:, D_MODEL + h * dh:D_MODEL + (h + 1) * dh] for h in range(heads)],
                      state, _later_matrix(seq), s_idx < t_idx)
    later = _later_matrix(blk)

    def cond(carry):
        n, alive, _ = carry
        return jnp.logical_and(n < n_blocks, alive)

    def body(carry):
        n, _, state = carry
        slot = n % 2
        wait(n, slot)

        @pl.when(n + 1 < n_blocks)
        def _():
            start(n + 1, 1 - slot)

        state = _sb_visit(
            qs, [kbuf[slot, pl.ds(h, blk, stride=heads), :].astype(BF16) for h in range(heads)],
            [vbuf[slot, pl.ds(h, blk, stride=heads), :].astype(BF16) for h in range(heads)],
            state, later, None)
        return n + 1, _sb_alive(state), state

    n_done, _, state = lax.while_loop(cond, body, (jnp.int32(0), _sb_alive(state), state))

    @pl.when(n_done < n_blocks)
    def _():
        wait(n_done, n_done % 2)

    for h in range(heads):
        o_ref[:, cols[h]] = state[h][1].astype(o_ref.dtype)


def _sb_cached(q, kv, cache_k, cache_v, batch, seq):
    dh = SB_HEAD_DIM
    blk = SB_BLOCK
    rows = blk * SB_HEADS
    n_blocks = cache_k.shape[1] // rows
    return pl.pallas_call(
        functools.partial(_sb_cached_kernel, seq=seq, blk=blk, n_blocks=n_blocks),
        grid=(batch,),
        in_specs=[pl.BlockSpec((seq, D_MODEL), lambda b: (b, 0)),
                  pl.BlockSpec((seq, 2 * D_MODEL), lambda b: (b, 0)),
                  pl.BlockSpec(memory_space=pl.ANY),
                  pl.BlockSpec(memory_space=pl.ANY)],
        out_specs=pl.BlockSpec((seq, D_MODEL), lambda b: (b, 0)),
        out_shape=jax.ShapeDtypeStruct((batch * seq, D_MODEL), BF16),
        scratch_shapes=[pltpu.VMEM((2, rows, dh), F32), pltpu.VMEM((2, rows, dh), F32),
                        pltpu.SemaphoreType.DMA((2, 2))],
        compiler_params=_params(1),
        name="sb_attention_cached",
    )(q, kv, cache_k, cache_v)


def _chunk_cols(w):
    lead = w.shape[:-1]
    w = w.reshape(lead + (FFN_CHUNKS, FFN_COLS))
    return jnp.moveaxis(w, -2, 0)


def _prep_weights(g_mix_norm, w_mlstm_in, b_mlstm_gates, g_mlstm_out, w_mlstm_out,
                  g_kv_norm, w_kv, w_sb_q, w_sb_o,
                  g_ffn_norm, w_ffn_up, w_ffn_conv, b_ffn_conv, w_ffn_down, g_final):
    main = 2 * MLSTM_HEADS * MLSTM_QK_DIM + MLSTM_HEADS * MLSTM_V_DIM + D_MODEL
    n_gate = 2 * MLSTM_HEADS
    pad = LANES - n_gate
    return dict(
        g_mix=g_mix_norm.reshape(DEPTH, 1, D_MODEL),
        w_in=w_mlstm_in[:, :, :main].astype(BF16),
        w_gate=jnp.pad(w_mlstm_in[:, :, main:], ((0, 0), (0, 0), (0, pad))).astype(BF16),
        b_gate=jnp.pad(b_mlstm_gates, ((0, 0), (0, pad))).reshape(N_A_LAYERS, 1, LANES),
        g_out=g_mlstm_out.reshape(N_A_LAYERS, 1, D_MODEL),
        w_out=w_mlstm_out.astype(BF16),
        g_kv=g_kv_norm.reshape(1, D_MODEL),
        w_kv=w_kv.astype(BF16),
        w_q=w_sb_q.astype(BF16),
        w_o=w_sb_o.astype(BF16),
        g_ffn=g_ffn_norm.reshape(DEPTH, 1, D_MODEL),
        w_up=w_ffn_up.reshape(DEPTH, D_MODEL, 2, FFN_CHUNKS, FFN_COLS).transpose(0, 2, 3, 1, 4)
        .astype(BF16),
        w_c=jnp.stack([_chunk_cols(w_ffn_conv[l]) for l in range(DEPTH)]),
        b_c=jnp.stack([_chunk_cols(b_ffn_conv[l].reshape(1, D_FF)) for l in range(DEPTH)]),
        w_d=w_ffn_down.reshape(DEPTH, FFN_CHUNKS, FFN_COLS, D_MODEL).astype(BF16),
        g_final=g_final.reshape(1, D_MODEL),
    )


def _trunk(x, c0, n0, m0, conv0, cache, w, batch, seq):
    new_c, new_n, new_m, new_conv = [], [], [], []
    kv_bf = k_new = v_new = None
    for l in range(DEPTH):
        if l < N_A_LAYERS:
            proj, gates = _mlstm_in(x, w["g_mix"][l], w["w_in"][l], w["w_gate"][l], w["b_gate"][l])
            mixed, c, n, m = _mlstm_scan(
                proj, gates, w["g_out"][l],
                c0[l].reshape(batch, MLSTM_PAIRS, LANES, LANES),
                jnp.broadcast_to(n0[l].reshape(batch, MLSTM_PAIRS, LANES, 1),
                                 (batch, MLSTM_PAIRS, LANES, LANES)),
                jnp.broadcast_to(m0[l].reshape(batch, MLSTM_HEADS, 1), (batch, MLSTM_HEADS, LANES)),
                batch, seq)
            w_mix = w["w_out"][l]
            new_c.append(c.reshape(batch, MLSTM_HEADS, MLSTM_QK_DIM, MLSTM_V_DIM))
            new_n.append(n[..., 0].reshape(batch, MLSTM_HEADS, MLSTM_QK_DIM))
            new_m.append(m[..., 0])
        else:
            j = l - N_A_LAYERS
            (q,) = _norm_matmul(x, w["g_mix"][l], w["w_q"][j], [BF16], "sb_q_proj")
            if cache is None:
                mixed = _sb_prompt(q, kv_bf, batch, seq)
            else:
                mixed = _sb_cached(q, kv_bf, cache[0], cache[1], batch, seq)
            w_mix = w["w_o"][j]
        prev = jnp.moveaxis(conv0[l].reshape(batch, 2, FFN_CHUNKS, FFN_COLS), 2, 0)
        x, tail = _ffn(x, mixed, w_mix, w["g_ffn"][l], w["w_up"], w["w_c"][l], w["b_c"][l],
                       w["w_d"], prev, w["g_final"], l, seq, l == DEPTH - 1)
        new_conv.append(jnp.moveaxis(tail, 0, 2).reshape(batch, 2, D_FF))
        if l == N_A_LAYERS - 1:
            k_rows, v_rows, kv_bf = _kv_proj(x, w["g_kv"], w["w_kv"])
            k_new = k_rows.reshape(batch, seq, SB_HEADS, SB_HEAD_DIM)
            v_new = v_rows.reshape(batch, seq, SB_HEADS, SB_HEAD_DIM)
    y = x.reshape(batch, seq, D_MODEL)
    return (y, jnp.stack(new_c), jnp.stack(new_n), jnp.stack(new_m), jnp.stack(new_conv),
            k_new, v_new)


def kernel(x_prompt, x_sample, cache_k, cache_v, state_mlstm_c, state_mlstm_n, state_mlstm_m,
           state_ffn_conv, g_mix_norm, w_mlstm_in, b_mlstm_gates, g_mlstm_out, w_mlstm_out,
           g_kv_norm, w_kv, w_sb_q, w_sb_o, g_ffn_norm, w_ffn_up, w_ffn_conv, b_ffn_conv,
           w_ffn_down, g_final):
    w = _prep_weights(g_mix_norm, w_mlstm_in, b_mlstm_gates, g_mlstm_out, w_mlstm_out,
                      g_kv_norm, w_kv, w_sb_q, w_sb_o,
                      g_ffn_norm, w_ffn_up, w_ffn_conv, b_ffn_conv, w_ffn_down, g_final)
    bp, tp, d = x_prompt.shape
    bs, ts, _ = x_sample.shape
    past = cache_k.shape[1]

    p_c0 = jnp.zeros((N_A_LAYERS, bp, MLSTM_HEADS, MLSTM_QK_DIM, MLSTM_V_DIM), F32)
    p_n0 = jnp.zeros((N_A_LAYERS, bp, MLSTM_HEADS, MLSTM_QK_DIM), F32)
    p_m0 = jnp.full((N_A_LAYERS, bp, MLSTM_HEADS), NEG_BIG, F32)
    p_conv0 = jnp.zeros((DEPTH, bp, 2, D_FF), F32)
    p_out = _trunk(x_prompt.reshape(bp * tp, d), p_c0, p_n0, p_m0, p_conv0, None, w, bp, tp)

    cache = (cache_k.reshape(bs, past * SB_HEADS, SB_HEAD_DIM),
             cache_v.reshape(bs, past * SB_HEADS, SB_HEAD_DIM))
    s_out = _trunk(x_sample.reshape(bs * ts, d), state_mlstm_c, state_mlstm_n, state_mlstm_m,
                   state_ffn_conv, cache, w, bs, ts)
    return (p_out[0], s_out[0]) + p_out[1:] + s_out[1:]
```

```python
import functools

import jax
import jax.numpy as jnp
from jax import lax
from jax.experimental import pallas as pl
from jax.experimental.pallas import tpu as pltpu

F32 = jnp.float32
BF16 = jnp.bfloat16

D_MODEL = 1024
DEPTH = 4
N_A_LAYERS = 2
MLSTM_HEADS = 8
MLSTM_QK_DIM = 64
MLSTM_V_DIM = 128
MLSTM_PAIRS = MLSTM_HEADS // 2
GATE_SOFTCAP = 15.0
SB_HEADS = 8
SB_HEAD_DIM = 128
D_FF = 2816
NORM_EPS = 1e-6
NEG_BIG = -1e30

LANES = 128
ROW_TILE = 512
FFN_SPLITS = ((0, 1024), (1024, 1024), (2048, 768))
MLSTM_CHUNK = 128
MLSTM_CHUNKS_PER_STEP = 2
SB_BLOCK = 256
SB_HEADS_PER_STEP = 8
SB_DEAD_LOG = -105.0
VMEM_LIMIT = 56 * 1024 * 1024


def _params(n_axes):
    return pltpu.CompilerParams(dimension_semantics=("arbitrary",) * n_axes,
                                vmem_limit_bytes=VMEM_LIMIT)


def _resident(shape):
    zeros = (0,) * len(shape)
    return pl.BlockSpec(shape, lambda *_: zeros, pipeline_mode=pl.Buffered(1))


def _rms(x, g):
    return x * lax.rsqrt(jnp.mean(x * x, axis=-1, keepdims=True) + NORM_EPS) * g


def _dot(a, b):
    return jnp.dot(a, b, preferred_element_type=F32)


def _dot_nt(a, b):
    return lax.dot_general(a, b, (((1,), (1,)), ((), ())), preferred_element_type=F32)


def _split3(x):
    hi = x.astype(BF16)
    r = x - hi.astype(F32)
    mid = r.astype(BF16)
    lo = (r - mid.astype(F32)).astype(BF16)
    return hi, mid, lo


def _log_sigmoid(z):
    return jnp.minimum(z, 0.0) - jnp.log(1.0 + jnp.exp(-jnp.abs(z)))


def _norm_matmul_kernel(x_ref, g_ref, w_ref, *o_refs, col_chunk):
    xn = _rms(x_ref[...], g_ref[...]).astype(BF16)
    n = w_ref.shape[1]
    for c0 in range(0, n, col_chunk):
        y = _dot(xn, w_ref[:, c0:c0 + col_chunk])
        for o_ref in o_refs:
            o_ref[:, c0:c0 + col_chunk] = y.astype(o_ref.dtype)


def _norm_matmul(x, g, w, out_dtypes, name):
    rows, d = x.shape
    n = w.shape[1]
    tm = ROW_TILE
    return pl.pallas_call(
        functools.partial(_norm_matmul_kernel, col_chunk=512),
        grid=(rows // tm,),
        in_specs=[pl.BlockSpec((tm, d), lambda i: (i, 0)), _resident((1, d)), _resident((d, n))],
        out_specs=[pl.BlockSpec((tm, n), lambda i: (i, 0)) for _ in out_dtypes],
        out_shape=[jax.ShapeDtypeStruct((rows, n), dt) for dt in out_dtypes],
        compiler_params=_params(1),
        name=name,
    )(x, g, w)


def _kv_proj_kernel(x_ref, g_ref, w_ref, k_ref, v_ref, kv_ref):
    xn = _rms(x_ref[...], g_ref[...]).astype(BF16)
    tm = x_ref.shape[0]
    group = 4
    for dst, base in ((k_ref, 0), (v_ref, D_MODEL)):
        for h0 in range(0, SB_HEADS, group):
            c0 = base + h0 * SB_HEAD_DIM
            y = _dot(xn, w_ref[:, c0:c0 + group * SB_HEAD_DIM])
            kv_ref[:, c0:c0 + group * SB_HEAD_DIM] = y.astype(kv_ref.dtype)
            for h in range(group):
                dst[pl.ds(h0 + h, tm, stride=SB_HEADS), :] = y[:, h * SB_HEAD_DIM:(h + 1) * SB_HEAD_DIM]


def _kv_proj(x, g, w):
    rows, d = x.shape
    tm = ROW_TILE
    head_rows = pl.BlockSpec((tm * SB_HEADS, SB_HEAD_DIM), lambda i: (i, 0))
    return pl.pallas_call(
        _kv_proj_kernel,
        grid=(rows // tm,),
        in_specs=[pl.BlockSpec((tm, d), lambda i: (i, 0)), _resident((1, d)), _resident((d, 2 * d))],
        out_specs=[head_rows, head_rows, pl.BlockSpec((tm, 2 * d), lambda i: (i, 0))],
        out_shape=[jax.ShapeDtypeStruct((rows * SB_HEADS, SB_HEAD_DIM), F32),
                   jax.ShapeDtypeStruct((rows * SB_HEADS, SB_HEAD_DIM), F32),
                   jax.ShapeDtypeStruct((rows, 2 * d), BF16)],
        compiler_params=_params(1),
        name="kv_proj",
    )(x, g, w)


def _mlstm_in_kernel(x_ref, g_ref, w_ref, wg_ref, bg_ref, o_ref, gate_ref, *, col_chunk):
    xn = _rms(x_ref[...], g_ref[...]).astype(BF16)
    n = w_ref.shape[1]
    for c0 in range(0, n, col_chunk):
        o_ref[:, c0:c0 + col_chunk] = _dot(xn, w_ref[:, c0:c0 + col_chunk]).astype(o_ref.dtype)
    pre = _dot(xn, wg_ref[...]) + bg_ref[...]
    capped = GATE_SOFTCAP * jnp.tanh(pre / GATE_SOFTCAP)
    lane = lax.broadcasted_iota(jnp.int32, capped.shape, 1)
    gate_ref[...] = jnp.where(lane < MLSTM_HEADS, capped, _log_sigmoid(capped))


def _mlstm_in(x, g, w, wg, bg):
    rows, d = x.shape
    n = w.shape[1]
    tm = ROW_TILE
    return pl.pallas_call(
        functools.partial(_mlstm_in_kernel, col_chunk=512),
        grid=(rows // tm,),
        in_specs=[pl.BlockSpec((tm, d), lambda i: (i, 0)), _resident((1, d)), _resident((d, n)),
                  _resident((d, LANES)), _resident((1, LANES))],
        out_specs=[pl.BlockSpec((tm, n), lambda i: (i, 0)),
                   pl.BlockSpec((tm, LANES), lambda i: (i, 0))],
        out_shape=[jax.ShapeDtypeStruct((rows, n), BF16),
                   jax.ShapeDtypeStruct((rows, LANES), F32)],
        compiler_params=_params(1),
        name="mlstm_in_proj",
    )(x, g, w, wg, bg)


def _transpose_rows(a):
    rows = a.shape[0]
    if rows < LANES:
        a = jnp.concatenate([a, jnp.zeros((LANES - rows, LANES), F32)], axis=0)
        return a.T[:, :rows]
    return a.T


def _mlstm_chunk(rows, q_ref, k_ref, v_ref, o_ref, gate_ref, gout_ref, h_ref, c_s, n_s, m_s):
    L = rows.stop - rows.start
    gates = gate_ref[rows, :]
    t_idx = lax.broadcasted_iota(jnp.int32, (L, L), 0)
    s_idx = lax.broadcasted_iota(jnp.int32, (L, L), 1)
    causal = s_idx <= t_idx
    tri = jnp.where(causal, 1.0, 0.0).astype(BF16)
    g_hi, g_mid, g_lo = _split3(gates)
    cum = _dot(tri, g_hi) + _dot(tri, g_mid) + _dot(tri, g_lo)
    gates_t = _transpose_rows(gates)
    cum_t = _transpose_rows(cum)
    lane = lax.broadcasted_iota(jnp.int32, (L, LANES), 1)
    sub = lax.broadcasted_iota(jnp.int32, (LANES, LANES), 0)
    ones = jnp.ones((2 * L, LANES), BF16)
    scale = MLSTM_QK_DIM ** -0.5

    def row_sum(x):
        hi = x.astype(BF16)
        lo = (x - hi.astype(F32)).astype(BF16)
        width = x.shape[1]
        return _dot(hi, ones[:width]) + _dot(lo, ones[:width])

    heads = range(MLSTM_HEADS)
    pairs = range(MLSTM_PAIRS)
    halves = [(lane // MLSTM_QK_DIM) == e for e in range(2)]
    k_pairs = [k_ref[rows, p * LANES:(p + 1) * LANES] for p in pairs]
    c_pairs = [c_s[p] for p in pairs]
    n_pairs = [n_s[p] for p in pairs]
    v_heads = [v_ref[rows, h * MLSTM_V_DIM:(h + 1) * MLSTM_V_DIM] for h in heads]
    m_prevs = [m_s[h:h + 1, :] for h in heads]

    qk, qc, qn = [], [], []
    for p in pairs:
        q_f32 = q_ref[rows, p * LANES:(p + 1) * LANES].astype(F32)
        c_bf = c_pairs[p].astype(BF16)
        n_bf = n_pairs[p].astype(BF16)
        for e in range(2):
            qm = jnp.where(halves[e], q_f32, 0.0).astype(BF16)
            qk.append(_dot_nt(qm, k_pairs[p]))
            qc.append(_dot(qm, c_bf))
            qn.append(_dot(qm, n_bf))

    lis, bs, m_ts, w_inters, ss = [], [], [], [], []
    for h in heads:
        li = jnp.broadcast_to(gates[:, h:h + 1], (L, LANES))
        b = jnp.broadcast_to(cum[:, MLSTM_HEADS + h:MLSTM_HEADS + h + 1], (L, LANES))
        li_row = gates_t[h:h + 1, :]
        b_row = cum_t[MLSTM_HEADS + h:MLSTM_HEADS + h + 1, :]
        log_d = jnp.where(causal, b[:, :L] + (li_row - b_row), NEG_BIG)
        m_inter = b + m_prevs[h]
        m_t = jnp.maximum(m_inter, jnp.max(log_d, axis=-1, keepdims=True))
        lis.append(li)
        bs.append(b)
        m_ts.append(m_t)
        w_inters.append(jnp.exp(m_inter - m_t))
        ss.append(qk[h] * scale * jnp.exp(log_d - m_t[:, :L]))

    sv = [_dot(ss[h].astype(BF16), v_heads[h]) for h in heads]
    s_sum = [row_sum(ss[h]) for h in heads]

    hs = []
    for h in heads:
        num = sv[h] + w_inters[h] * (qc[h] * scale)
        den = s_sum[h] + w_inters[h] * (qn[h] * scale)
        hs.append(num / jnp.maximum(jnp.abs(den), jnp.exp(-m_ts[h])))
    sq_sum = [row_sum(hs[h] * hs[h]) for h in heads]

    kws, decays = [], []
    for h in heads:
        cols = slice(h * MLSTM_V_DIM, (h + 1) * MLSTM_V_DIM)
        hh = hs[h] * lax.rsqrt(sq_sum[h] * (1.0 / MLSTM_V_DIM) + NORM_EPS)
        hh = hh * gout_ref[:, cols] * jax.nn.sigmoid(o_ref[rows, cols].astype(F32))
        h_ref[rows, cols] = hh.astype(h_ref.dtype)
        m_new = m_ts[h][L - 1:L, :]
        b_last = bs[h][L - 1:L, :]
        decays.append(jnp.exp(b_last + m_prevs[h] - m_new))
        w_in = jnp.exp(b_last - bs[h] + lis[h] - m_new)
        kws.append(jnp.where(halves[h % 2], k_pairs[h // 2].astype(F32) * w_in, 0.0))
        m_s[h:h + 1, :] = m_new

    for p in pairs:
        kw = jnp.concatenate([kws[2 * p], kws[2 * p + 1]], axis=0)
        vv = jnp.concatenate([v_heads[2 * p], v_heads[2 * p + 1]], axis=0)
        kw_t = kw.T.astype(BF16)
        decay = jnp.where(sub < MLSTM_QK_DIM, decays[2 * p], decays[2 * p + 1])
        c_s[p] = decay * c_pairs[p] + _dot(kw_t, vv)
        n_s[p] = decay * n_pairs[p] + _dot(kw_t, ones)


def _mlstm_scan_kernel(q_ref, k_ref, v_ref, o_ref, gate_ref, gout_ref, c0_ref, n0_ref, m0_ref,
                       h_ref, c_ref, n_ref, m_ref, c_s, n_s, m_s, *, chunk, chunks):
    j = pl.program_id(1)

    @pl.when(j == 0)
    def _():
        c_s[...] = c0_ref[0]
        n_s[...] = n0_ref[0]
        m_s[...] = m0_ref[0]

    for ci in range(chunks):
        _mlstm_chunk(slice(ci * chunk, (ci + 1) * chunk), q_ref, k_ref, v_ref, o_ref, gate_ref,
                     gout_ref, h_ref, c_s, n_s, m_s)

    @pl.when(j == pl.num_programs(1) - 1)
    def _():
        c_ref[0] = c_s[...]
        n_ref[0] = n_s[...]
        m_ref[0] = m_s[...]


def _mlstm_scan(proj, gates, g_out, c0, n0, m0, batch, seq):
    L = min(MLSTM_CHUNK, seq)
    chunks = min(MLSTM_CHUNKS_PER_STEP, seq // L)
    R = L * chunks
    nchunk = seq // R
    d = D_MODEL
    qk = MLSTM_HEADS * MLSTM_QK_DIM

    def row(b, j):
        return b * nchunk + j

    return pl.pallas_call(
        functools.partial(_mlstm_scan_kernel, chunk=L, chunks=chunks),
        grid=(batch, nchunk),
        in_specs=[
            pl.BlockSpec((R, qk), lambda b, j: (row(b, j), 0)),
            pl.BlockSpec((R, qk), lambda b, j: (row(b, j), 1)),
            pl.BlockSpec((R, d), lambda b, j: (row(b, j), 1)),
            pl.BlockSpec((R, d), lambda b, j: (row(b, j), 2)),
            pl.BlockSpec((R, LANES), lambda b, j: (row(b, j), 0)),
            _resident((1, d)),
            pl.BlockSpec((1, MLSTM_PAIRS, LANES, LANES), lambda b, j: (b, 0, 0, 0)),
            pl.BlockSpec((1, MLSTM_PAIRS, LANES, LANES), lambda b, j: (b, 0, 0, 0)),
            pl.BlockSpec((1, MLSTM_HEADS, LANES), lambda b, j: (b, 0, 0)),
        ],
        out_specs=[
            pl.BlockSpec((R, d), lambda b, j: (row(b, j), 0)),
            pl.BlockSpec((1, MLSTM_PAIRS, LANES, LANES), lambda b, j: (b, 0, 0, 0)),
            pl.BlockSpec((1, MLSTM_PAIRS, LANES, LANES), lambda b, j: (b, 0, 0, 0)),
            pl.BlockSpec((1, MLSTM_HEADS, LANES), lambda b, j: (b, 0, 0)),
        ],
        out_shape=[
            jax.ShapeDtypeStruct((batch * seq, d), BF16),
            jax.ShapeDtypeStruct((batch, MLSTM_PAIRS, LANES, LANES), F32),
            jax.ShapeDtypeStruct((batch, MLSTM_PAIRS, LANES, LANES), F32),
            jax.ShapeDtypeStruct((batch, MLSTM_HEADS, LANES), F32),
        ],
        scratch_shapes=[pltpu.VMEM((MLSTM_PAIRS, LANES, LANES), F32),
                        pltpu.VMEM((MLSTM_PAIRS, LANES, LANES), F32),
                        pltpu.VMEM((MLSTM_HEADS, LANES), F32)],
        compiler_params=_params(2),
        name="mlstm_scan",
    )(proj, proj, proj, proj, gates, g_out, c0, n0, m0)


def _ffn_kernel(x_ref, y_ref, wo_ref, g_ref, wu_ref, wc_ref, bc_ref, wd_ref, prev_ref, gf_ref,
                o_ref, tail_ref, xn_s, acc_s, carry_s, a_s, v_s, *, seq, tm, final_norm):
    sub_len = min(seq, tm)
    n_sub = tm // sub_len
    tiles_per_seq = max(1, seq // tm)
    i = pl.program_id(0)
    starts_seq = (i % tiles_per_seq) == 0

    x = x_ref[...] + _dot(y_ref[...], wo_ref[...])
    xn_s[...] = _rms(x, g_ref[...]).astype(BF16)
    acc_s[...] = x
    row = lax.broadcasted_iota(jnp.int32, (sub_len, 1), 0)

    if n_sub == 1:
        @pl.when(starts_seq)
        def _():
            carry_s[...] = prev_ref[0]

    def up(k):
        c0, n = FFN_SPLITS[k]
        xn = xn_s[...]
        a_s[:, c0:c0 + n] = _dot(xn, wu_ref[:, c0:c0 + n])
        v_s[:, c0:c0 + n] = _dot(xn, wu_ref[:, D_FF + c0:D_FF + c0 + n])

    def down(k):
        c0, n = FFN_SPLITS[k]
        cols = slice(c0, c0 + n)
        w = wc_ref[:, cols]
        b = bc_ref[:, cols]
        hs = []
        for s in range(n_sub):
            rows = slice(s * sub_len, (s + 1) * sub_len)
            a = a_s[rows, cols]
            prev = carry_s[:, cols] if n_sub == 1 else prev_ref[s, :, cols]
            p0, p1 = prev[0:1], prev[1:2]
            am1 = jnp.where(row == 0, p1, pltpu.roll(a, 1, 0))
            am2 = jnp.where(row == 0, p0, jnp.where(row == 1, p1, pltpu.roll(a, 2, 0)))
            conv = ((b + w[0:1] * am2) + w[1:2] * am1) + w[2:3] * a
            hs.append((jax.nn.gelu(conv) * v_s[rows, cols]).astype(BF16))
            tail = a[sub_len - 2:sub_len]
            tail_ref[s, :, cols] = tail
            if n_sub == 1:
                carry_s[:, cols] = tail
        h = hs[0] if n_sub == 1 else jnp.concatenate(hs, axis=0)
        acc_s[...] += _dot(h, wd_ref[cols, :])

    up(0)
    for k in range(len(FFN_SPLITS)):
        if k + 1 < len(FFN_SPLITS):
            up(k + 1)
        down(k)
    out = acc_s[...]
    o_ref[...] = _rms(out, gf_ref[...]) if final_norm else out


def _ffn(x, y, wo, g, w_up, wc, bc, w_dn, prev, g_final, layer, seq, final_norm):
    rows, d = x.shape
    tm = ROW_TILE
    batch = rows // seq
    if seq >= tm:
        seqs_per_tile = 1
        tiles_per_seq = seq // tm
        seq_block = lambda i: (i // tiles_per_seq, 0, 0)
    else:
        seqs_per_tile = tm // seq
        seq_block = lambda i: (i, 0, 0)
    state_spec = pl.BlockSpec((seqs_per_tile, 2, D_FF), seq_block)
    up_spec = pl.BlockSpec((None, d, 2 * D_FF), lambda i: (layer, 0, 0), pipeline_mode=pl.Buffered(1))
    down_spec = pl.BlockSpec((None, D_FF, d), lambda i: (layer, 0, 0), pipeline_mode=pl.Buffered(1))
    return pl.pallas_call(
        functools.partial(_ffn_kernel, seq=seq, tm=tm, final_norm=final_norm),
        grid=(rows // tm,),
        in_specs=[pl.BlockSpec((tm, d), lambda i: (i, 0)), pl.BlockSpec((tm, d), lambda i: (i, 0)),
                  _resident((d, d)), _resident((1, d)), up_spec,
                  _resident((3, D_FF)), _resident((1, D_FF)), down_spec, state_spec,
                  _resident((1, d))],
        out_specs=[pl.BlockSpec((tm, d), lambda i: (i, 0)), state_spec],
        out_shape=[jax.ShapeDtypeStruct((rows, d), F32),
                   jax.ShapeDtypeStruct((batch, 2, D_FF), F32)],
        scratch_shapes=[pltpu.VMEM((tm, d), BF16), pltpu.VMEM((tm, d), F32),
                        pltpu.VMEM((2, D_FF), F32),
                        pltpu.VMEM((tm, D_FF), F32), pltpu.VMEM((tm, D_FF), F32)],
        compiler_params=_params(1),
        name="conv_ffn",
    )(x, y, wo, g, w_up, wc, bc, w_dn, prev, g_final)


def _later_matrix(bk):
    j_idx = lax.broadcasted_iota(jnp.int32, (bk, bk), 0)
    s_idx = lax.broadcasted_iota(jnp.int32, (bk, bk), 1)
    return jnp.where(j_idx > s_idx, 1.0, 0.0).astype(BF16)


def _sb_visit(qs, kbs, vbs, state, later, strict_lower):
    heads = range(len(qs))
    zs = [_dot_nt(qs[h], kbs[h]) for h in heads]
    log_betas, log_keeps, splits = [], [], []
    for h in heads:
        z = zs[h] * (SB_HEAD_DIM ** -0.5)
        log_beta = _log_sigmoid(z)
        log_keep = log_beta - z
        if strict_lower is not None:
            log_keep = jnp.where(strict_lower, log_keep, 0.0)
        hi = log_keep.astype(BF16)
        log_betas.append(log_beta)
        log_keeps.append(log_keep)
        splits.append((hi, (log_keep - hi.astype(F32)).astype(BF16)))
    afters = [_dot(hi, later) + _dot(lo, later) for hi, lo in splits]
    weights, runs = [], []
    for h in heads:
        a = jnp.exp(log_betas[h] + afters[h] + state[h][0])
        if strict_lower is not None:
            a = jnp.where(strict_lower, a, 0.0)
        weights.append(a.astype(BF16))
        runs.append(state[h][0] + afters[h][:, 0:1] + log_keeps[h][:, 0:1])
    return tuple((runs[h], state[h][1] + _dot(weights[h], vbs[h])) for h in heads)


def _sb_alive(state):
    top = state[0][0]
    for run, _ in state[1:]:
        top = jnp.maximum(top, run)
    return jnp.max(top) > SB_DEAD_LOG


def _sb_prompt_kernel(q_ref, k_ref, v_ref, o_ref, *, blk, heads):
    i = pl.program_id(2)
    dh = SB_HEAD_DIM
    cols = [slice(h * dh, (h + 1) * dh) for h in range(heads)]
    qs = [q_ref[:, c] for c in cols]
    t_idx = lax.broadcasted_iota(jnp.int32, (blk, blk), 0)
    s_idx = lax.broadcasted_iota(jnp.int32, (blk, blk), 1)
    later = _later_matrix(blk)

    def visit(off, state, mask):
        return _sb_visit(qs, [k_ref[pl.ds(off, blk), c] for c in cols],
                         [v_ref[pl.ds(off, blk), c] for c in cols], state, later, mask)

    state = tuple((jnp.zeros((blk, 1), F32), jnp.zeros((blk, dh), F32)) for _ in range(heads))
    state = visit(pl.multiple_of(i * blk, blk), state, s_idx < t_idx)

    def cond(carry):
        n, alive, _ = carry
        return jnp.logical_and(n < i, alive)

    def body(carry):
        n, _, state = carry
        state = visit(pl.multiple_of((i - 1 - n) * blk, blk), state, None)
        return n + 1, _sb_alive(state), state

    _, _, state = lax.while_loop(cond, body, (jnp.int32(0), _sb_alive(state), state))
    for h in range(heads):
        o_ref[:, cols[h]] = state[h][1].astype(o_ref.dtype)


def _sb_prompt(q, kv, batch, seq):
    blk = SB_BLOCK
    nq = seq // blk
    heads = SB_HEADS_PER_STEP
    groups = SB_HEADS // heads
    width = heads * SB_HEAD_DIM
    return pl.pallas_call(
        functools.partial(_sb_prompt_kernel, blk=blk, heads=heads),
        grid=(batch, groups, nq),
        in_specs=[pl.BlockSpec((blk, width), lambda b, g, i: (b * nq + i, g)),
                  pl.BlockSpec((seq, width), lambda b, g, i: (b, g)),
                  pl.BlockSpec((seq, width), lambda b, g, i: (b, groups + g))],
        out_specs=pl.BlockSpec((blk, width), lambda b, g, i: (b * nq + i, g)),
        out_shape=jax.ShapeDtypeStruct((batch * seq, D_MODEL), BF16),
        compiler_params=_params(3),
        name="sb_attention_prompt",
    )(q, kv, kv)


def _sb_cached_kernel(q_ref, kv_ref, kc_hbm, vc_hbm, o_ref, kbuf, vbuf, sems, *, seq, blk, n_blocks):
    b = pl.program_id(0)
    dh = SB_HEAD_DIM
    heads = SB_HEADS
    rows = blk * heads
    cols = [slice(h * dh, (h + 1) * dh) for h in range(heads)]
    qs = [q_ref[:, c] for c in cols]

    def copies(n, slot):
        off = pl.multiple_of((n_blocks - 1 - n) * rows, rows)
        return (pltpu.make_async_copy(kc_hbm.at[b, pl.ds(off, rows), :], kbuf.at[slot], sems.at[0, slot]),
                pltpu.make_async_copy(vc_hbm.at[b, pl.ds(off, rows), :], vbuf.at[slot], sems.at[1, slot]))

    def start(n, slot):
        for c in copies(n, slot):
            c.start()

    def wait(n, slot):
        for c in copies(n, slot):
            c.wait()

    start(0, 0)

    t_idx = lax.broadcasted_iota(jnp.int32, (seq, seq), 0)
    s_idx = lax.broadcasted_iota(jnp.int32, (seq, seq), 1)
    state = tuple((jnp.zeros((seq, 1), F32), jnp.zeros((seq, dh), F32)) for _ in range(heads))
    state = _sb_visit(qs, [kv_ref[:, c] for c in cols],
                      [kv_ref[:, D_MODEL + h * dh:D_MODEL + (h + 1) * dh] for h in range(heads)],
                      state, _later_matrix(seq), s_idx < t_idx)
    later = _later_matrix(blk)

    def cond(carry):
        n, alive, _ = carry
        return jnp.logical_and(n < n_blocks, alive)

    def body(carry):
        n, _, state = carry
        slot = n % 2
        wait(n, slot)

        @pl.when(n + 1 < n_blocks)
        def _():
            start(n + 1, 1 - slot)

        state = _sb_visit(
            qs, [kbuf[slot, pl.ds(h, blk, stride=heads), :].astype(BF16) for h in range(heads)],
            [vbuf[slot, pl.ds(h, blk, stride=heads), :].astype(BF16) for h in range(heads)],
            state, later, None)
        return n + 1, _sb_alive(state), state

    n_done, _, state = lax.while_loop(cond, body, (jnp.int32(0), _sb_alive(state), state))

    @pl.when(n_done < n_blocks)
    def _():
        wait(n_done, n_done % 2)

    for h in range(heads):
        o_ref[:, cols[h]] = state[h][1].astype(o_ref.dtype)


def _sb_cached(q, kv, cache_k, cache_v, batch, seq):
    dh = SB_HEAD_DIM
    blk = SB_BLOCK
    rows = blk * SB_HEADS
    n_blocks = cache_k.shape[1] // rows
    return pl.pallas_call(
        functools.partial(_sb_cached_kernel, seq=seq, blk=blk, n_blocks=n_blocks),
        grid=(batch,),
        in_specs=[pl.BlockSpec((seq, D_MODEL), lambda b: (b, 0)),
                  pl.BlockSpec((seq, 2 * D_MODEL), lambda b: (b, 0)),
                  pl.BlockSpec(memory_space=pl.ANY),
                  pl.BlockSpec(memory_space=pl.ANY)],
        out_specs=pl.BlockSpec((seq, D_MODEL), lambda b: (b, 0)),
        out_shape=jax.ShapeDtypeStruct((batch * seq, D_MODEL), BF16),
        scratch_shapes=[pltpu.VMEM((2, rows, dh), F32), pltpu.VMEM((2, rows, dh), F32),
                        pltpu.SemaphoreType.DMA((2, 2))],
        compiler_params=_params(1),
        name="sb_attention_cached",
    )(q, kv, cache_k, cache_v)


def _prep_weights(g_mix_norm, w_mlstm_in, b_mlstm_gates, g_mlstm_out, w_mlstm_out,
                  g_kv_norm, w_kv, w_sb_q, w_sb_o,
                  g_ffn_norm, w_ffn_up, w_ffn_conv, b_ffn_conv, w_ffn_down, g_final):
    main = 2 * MLSTM_HEADS * MLSTM_QK_DIM + MLSTM_HEADS * MLSTM_V_DIM + D_MODEL
    n_gate = 2 * MLSTM_HEADS
    pad = LANES - n_gate
    return dict(
        g_mix=g_mix_norm.reshape(DEPTH, 1, D_MODEL),
        w_in=w_mlstm_in[:, :, :main].astype(BF16),
        w_gate=jnp.pad(w_mlstm_in[:, :, main:], ((0, 0), (0, 0), (0, pad))).astype(BF16),
        b_gate=jnp.pad(b_mlstm_gates, ((0, 0), (0, pad))).reshape(N_A_LAYERS, 1, LANES),
        g_out=g_mlstm_out.reshape(N_A_LAYERS, 1, D_MODEL),
        w_out=w_mlstm_out.astype(BF16),
        g_kv=g_kv_norm.reshape(1, D_MODEL),
        w_kv=w_kv.astype(BF16),
        w_q=w_sb_q.astype(BF16),
        w_o=w_sb_o.astype(BF16),
        g_ffn=g_ffn_norm.reshape(DEPTH, 1, D_MODEL),
        w_up=w_ffn_up.astype(BF16),
        w_c=w_ffn_conv,
        b_c=b_ffn_conv.reshape(DEPTH, 1, D_FF),
        w_d=w_ffn_down.astype(BF16),
        g_final=g_final.reshape(1, D_MODEL),
    )


def _trunk(x, c0, n0, m0, conv0, cache, w, batch, seq):
    new_c, new_n, new_m, new_conv = [], [], [], []
    kv_bf = k_new = v_new = None
    for l in range(DEPTH):
        if l < N_A_LAYERS:
            proj, gates = _mlstm_in(x, w["g_mix"][l], w["w_in"][l], w["w_gate"][l], w["b_gate"][l])
            mixed, c, n, m = _mlstm_scan(
                proj, gates, w["g_out"][l],
                c0[l].reshape(batch, MLSTM_PAIRS, LANES, LANES),
                jnp.broadcast_to(n0[l].reshape(batch, MLSTM_PAIRS, LANES, 1),
                                 (batch, MLSTM_PAIRS, LANES, LANES)),
                jnp.broadcast_to(m0[l].reshape(batch, MLSTM_HEADS, 1), (batch, MLSTM_HEADS, LANES)),
                batch, seq)
            w_mix = w["w_out"][l]
            new_c.append(c.reshape(batch, MLSTM_HEADS, MLSTM_QK_DIM, MLSTM_V_DIM))
            new_n.append(n[..., 0].reshape(batch, MLSTM_HEADS, MLSTM_QK_DIM))
            new_m.append(m[..., 0])
        else:
            j = l - N_A_LAYERS
            (q,) = _norm_matmul(x, w["g_mix"][l], w["w_q"][j], [BF16], "sb_q_proj")
            if cache is None:
                mixed = _sb_prompt(q, kv_bf, batch, seq)
            else:
                mixed = _sb_cached(q, kv_bf, cache[0], cache[1], batch, seq)
            w_mix = w["w_o"][j]
        x, tail = _ffn(x, mixed, w_mix, w["g_ffn"][l], w["w_up"], w["w_c"][l], w["b_c"][l],
                       w["w_d"], conv0[l], w["g_final"], l, seq, l == DEPTH - 1)
        new_conv.append(tail)
        if l == N_A_LAYERS - 1:
            k_rows, v_rows, kv_bf = _kv_proj(x, w["g_kv"], w["w_kv"])
            k_new = k_rows.reshape(batch, seq, SB_HEADS, SB_HEAD_DIM)
            v_new = v_rows.reshape(batch, seq, SB_HEADS, SB_HEAD_DIM)
    y = x.reshape(batch, seq, D_MODEL)
    return (y, jnp.stack(new_c), jnp.stack(new_n), jnp.stack(new_m), jnp.stack(new_conv),
            k_new, v_new)


def kernel(x_prompt, x_sample, cache_k, cache_v, state_mlstm_c, state_mlstm_n, state_mlstm_m,
           state_ffn_conv, g_mix_norm, w_mlstm_in, b_mlstm_gates, g_mlstm_out, w_mlstm_out,
           g_kv_norm, w_kv, w_sb_q, w_sb_o, g_ffn_norm, w_ffn_up, w_ffn_conv, b_ffn_conv,
           w_ffn_down, g_final):
    w = _prep_weights(g_mix_norm, w_mlstm_in, b_mlstm_gates, g_mlstm_out, w_mlstm_out,
                      g_kv_norm, w_kv, w_sb_q, w_sb_o,
                      g_ffn_norm, w_ffn_up, w_ffn_conv, b_ffn_conv, w_ffn_down, g_final)
    bp, tp, d = x_prompt.shape
    bs, ts, _ = x_sample.shape
    past = cache_k.shape[1]

    p_c0 = jnp.zeros((N_A_LAYERS, bp, MLSTM_HEADS, MLSTM_QK_DIM, MLSTM_V_DIM), F32)
    p_n0 = jnp.zeros((N_A_LAYERS, bp, MLSTM_HEADS, MLSTM_QK_DIM), F32)
    p_m0 = jnp.full((N_A_LAYERS, bp, MLSTM_HEADS), NEG_BIG, F32)
    p_conv0 = jnp.zeros((DEPTH, bp, 2, D_FF), F32)
    p_out = _trunk(x_prompt.reshape(bp * tp, d), p_c0, p_n0, p_m0, p_conv0, None, w, bp, tp)

    cache = (cache_k.reshape(bs, past * SB_HEADS, SB_HEAD_DIM),
             cache_v.reshape(bs, past * SB_HEADS, SB_HEAD_DIM))
    s_out = _trunk(x_sample.reshape(bs * ts, d), state_mlstm_c, state_mlstm_n, state_mlstm_m,
                   state_ffn_conv, cache, w, bs, ts)
    return (p_out[0], s_out[0]) + p_out[1:] + s_out[1:]
```

```python
import functools

import jax
import jax.numpy as jnp
from jax import lax
from jax.experimental import pallas as pl
from jax.experimental.pallas import tpu as pltpu

F32 = jnp.float32
BF16 = jnp.bfloat16

D_MODEL = 1024
DEPTH = 4
N_A_LAYERS = 2
MLSTM_HEADS = 8
MLSTM_QK_DIM = 64
MLSTM_V_DIM = 128
MLSTM_PAIRS = MLSTM_HEADS // 2
GATE_SOFTCAP = 15.0
SB_HEADS = 8
SB_HEAD_DIM = 128
D_FF = 2816
NORM_EPS = 1e-6
NEG_BIG = -1e30

LANES = 128
ROW_TILE = 512
FFN_SPLITS = ((0, 1024), (1024, 1024), (2048, 768))
MLSTM_CHUNK = 128
MLSTM_CHUNKS_PER_STEP = 4
SB_BLOCK = 256
SB_HEADS_PER_STEP = 8
SB_DEAD_LOG = -105.0
VMEM_LIMIT = 56 * 1024 * 1024


def _params(n_axes):
    return pltpu.CompilerParams(dimension_semantics=("arbitrary",) * n_axes,
                                vmem_limit_bytes=VMEM_LIMIT)


def _resident(shape):
    zeros = (0,) * len(shape)
    return pl.BlockSpec(shape, lambda *_: zeros, pipeline_mode=pl.Buffered(1))


def _rms(x, g):
    return x * lax.rsqrt(jnp.mean(x * x, axis=-1, keepdims=True) + NORM_EPS) * g


def _dot(a, b):
    return jnp.dot(a, b, preferred_element_type=F32)


def _dot_nt(a, b):
    return lax.dot_general(a, b, (((1,), (1,)), ((), ())), preferred_element_type=F32)


def _split3(x):
    hi = x.astype(BF16)
    r = x - hi.astype(F32)
    mid = r.astype(BF16)
    lo = (r - mid.astype(F32)).astype(BF16)
    return hi, mid, lo


def _log_sigmoid(z):
    return jnp.minimum(z, 0.0) - jnp.log(1.0 + jnp.exp(-jnp.abs(z)))


def _norm_matmul_kernel(x_ref, g_ref, w_ref, *o_refs, col_chunk):
    xn = _rms(x_ref[...], g_ref[...]).astype(BF16)
    n = w_ref.shape[1]
    for c0 in range(0, n, col_chunk):
        y = _dot(xn, w_ref[:, c0:c0 + col_chunk])
        for o_ref in o_refs:
            o_ref[:, c0:c0 + col_chunk] = y.astype(o_ref.dtype)


def _norm_matmul(x, g, w, out_dtypes, name):
    rows, d = x.shape
    n = w.shape[1]
    tm = ROW_TILE
    return pl.pallas_call(
        functools.partial(_norm_matmul_kernel, col_chunk=512),
        grid=(rows // tm,),
        in_specs=[pl.BlockSpec((tm, d), lambda i: (i, 0)), _resident((1, d)), _resident((d, n))],
        out_specs=[pl.BlockSpec((tm, n), lambda i: (i, 0)) for _ in out_dtypes],
        out_shape=[jax.ShapeDtypeStruct((rows, n), dt) for dt in out_dtypes],
        compiler_params=_params(1),
        name=name,
    )(x, g, w)


def _kv_proj_kernel(x_ref, g_ref, w_ref, k_ref, v_ref, kv_ref):
    xn = _rms(x_ref[...], g_ref[...]).astype(BF16)
    tm = x_ref.shape[0]
    group = 4
    for dst, base in ((k_ref, 0), (v_ref, D_MODEL)):
        for h0 in range(0, SB_HEADS, group):
            c0 = base + h0 * SB_HEAD_DIM
            y = _dot(xn, w_ref[:, c0:c0 + group * SB_HEAD_DIM])
            kv_ref[:, c0:c0 + group * SB_HEAD_DIM] = y.astype(kv_ref.dtype)
            for h in range(group):
                dst[pl.ds(h0 + h, tm, stride=SB_HEADS), :] = y[:, h * SB_HEAD_DIM:(h + 1) * SB_HEAD_DIM]


def _kv_proj(x, g, w):
    rows, d = x.shape
    tm = ROW_TILE
    head_rows = pl.BlockSpec((tm * SB_HEADS, SB_HEAD_DIM), lambda i: (i, 0))
    return pl.pallas_call(
        _kv_proj_kernel,
        grid=(rows // tm,),
        in_specs=[pl.BlockSpec((tm, d), lambda i: (i, 0)), _resident((1, d)), _resident((d, 2 * d))],
        out_specs=[head_rows, head_rows, pl.BlockSpec((tm, 2 * d), lambda i: (i, 0))],
        out_shape=[jax.ShapeDtypeStruct((rows * SB_HEADS, SB_HEAD_DIM), F32),
                   jax.ShapeDtypeStruct((rows * SB_HEADS, SB_HEAD_DIM), F32),
                   jax.ShapeDtypeStruct((rows, 2 * d), BF16)],
        compiler_params=_params(1),
        name="kv_proj",
    )(x, g, w)


def _mlstm_in_kernel(x_ref, g_ref, w_ref, wg_ref, bg_ref, o_ref, gate_ref, *, col_chunk):
    xn = _rms(x_ref[...], g_ref[...]).astype(BF16)
    n = w_ref.shape[1]
    for c0 in range(0, n, col_chunk):
        o_ref[:, c0:c0 + col_chunk] = _dot(xn, w_ref[:, c0:c0 + col_chunk]).astype(o_ref.dtype)
    pre = _dot(xn, wg_ref[...]) + bg_ref[...]
    capped = GATE_SOFTCAP * jnp.tanh(pre / GATE_SOFTCAP)
    lane = lax.broadcasted_iota(jnp.int32, capped.shape, 1)
    gate_ref[...] = jnp.where(lane < MLSTM_HEADS, capped, _log_sigmoid(capped))


def _mlstm_in(x, g, w, wg, bg):
    rows, d = x.shape
    n = w.shape[1]
    tm = ROW_TILE
    return pl.pallas_call(
        functools.partial(_mlstm_in_kernel, col_chunk=512),
        grid=(rows // tm,),
        in_specs=[pl.BlockSpec((tm, d), lambda i: (i, 0)), _resident((1, d)), _resident((d, n)),
                  _resident((d, LANES)), _resident((1, LANES))],
        out_specs=[pl.BlockSpec((tm, n), lambda i: (i, 0)),
                   pl.BlockSpec((tm, LANES), lambda i: (i, 0))],
        out_shape=[jax.ShapeDtypeStruct((rows, n), BF16),
                   jax.ShapeDtypeStruct((rows, LANES), F32)],
        compiler_params=_params(1),
        name="mlstm_in_proj",
    )(x, g, w, wg, bg)


def _transpose_rows(a):
    rows = a.shape[0]
    if rows < LANES:
        a = jnp.concatenate([a, jnp.zeros((LANES - rows, LANES), F32)], axis=0)
        return a.T[:, :rows]
    return a.T


def _mlstm_chunk(rows, q_ref, k_ref, v_ref, o_ref, gate_ref, gout_ref, h_ref, c_s, n_s, m_s):
    L = rows.stop - rows.start
    gates = gate_ref[rows, :]
    t_idx = lax.broadcasted_iota(jnp.int32, (L, L), 0)
    s_idx = lax.broadcasted_iota(jnp.int32, (L, L), 1)
    causal = s_idx <= t_idx
    tri = jnp.where(causal, 1.0, 0.0).astype(BF16)
    g_hi, g_mid, g_lo = _split3(gates)
    cum = _dot(tri, g_hi) + _dot(tri, g_mid) + _dot(tri, g_lo)
    gates_t = _transpose_rows(gates)
    cum_t = _transpose_rows(cum)
    lane = lax.broadcasted_iota(jnp.int32, (L, LANES), 1)
    sub = lax.broadcasted_iota(jnp.int32, (LANES, LANES), 0)
    ones = jnp.ones((2 * L, LANES), BF16)
    scale = MLSTM_QK_DIM ** -0.5

    def row_sum(x):
        hi = x.astype(BF16)
        lo = (x - hi.astype(F32)).astype(BF16)
        width = x.shape[1]
        return _dot(hi, ones[:width]) + _dot(lo, ones[:width])

    heads = range(MLSTM_HEADS)
    pairs = range(MLSTM_PAIRS)
    halves = [(lane // MLSTM_QK_DIM) == e for e in range(2)]
    k_pairs = [k_ref[rows, p * LANES:(p + 1) * LANES] for p in pairs]
    c_pairs = [c_s[p] for p in pairs]
    n_pairs = [n_s[p] for p in pairs]
    v_heads = [v_ref[rows, h * MLSTM_V_DIM:(h + 1) * MLSTM_V_DIM] for h in heads]
    m_prevs = [m_s[h:h + 1, :] for h in heads]

    qk, qc, qn = [], [], []
    for p in pairs:
        q_f32 = q_ref[rows, p * LANES:(p + 1) * LANES].astype(F32)
        c_bf = c_pairs[p].astype(BF16)
        n_bf = n_pairs[p].astype(BF16)
        for e in range(2):
            qm = jnp.where(halves[e], q_f32, 0.0).astype(BF16)
            qk.append(_dot_nt(qm, k_pairs[p]))
            qc.append(_dot(qm, c_bf))
            qn.append(_dot(qm, n_bf))

    lis, bs, m_ts, w_inters, ss = [], [], [], [], []
    for h in heads:
        li = jnp.broadcast_to(gates[:, h:h + 1], (L, LANES))
        b = jnp.broadcast_to(cum[:, MLSTM_HEADS + h:MLSTM_HEADS + h + 1], (L, LANES))
        li_row = gates_t[h:h + 1, :]
        b_row = cum_t[MLSTM_HEADS + h:MLSTM_HEADS + h + 1, :]
        log_d = jnp.where(causal, b[:, :L] + (li_row - b_row), NEG_BIG)
        m_inter = b + m_prevs[h]
        m_t = jnp.maximum(m_inter, jnp.max(log_d, axis=-1, keepdims=True))
        lis.append(li)
        bs.append(b)
        m_ts.append(m_t)
        w_inters.append(jnp.exp(m_inter - m_t))
        ss.append(qk[h] * scale * jnp.exp(log_d - m_t[:, :L]))

    sv = [_dot(ss[h].astype(BF16), v_heads[h]) for h in heads]
    s_sum = [row_sum(ss[h]) for h in heads]

    hs = []
    for h in heads:
        num = sv[h] + w_inters[h] * (qc[h] * scale)
        den = s_sum[h] + w_inters[h] * (qn[h] * scale)
        hs.append(num / jnp.maximum(jnp.abs(den), jnp.exp(-m_ts[h])))
    sq_sum = [row_sum(hs[h] * hs[h]) for h in heads]

    kws, decays = [], []
    for h in heads:
        cols = slice(h * MLSTM_V_DIM, (h + 1) * MLSTM_V_DIM)
        hh = hs[h] * lax.rsqrt(sq_sum[h] * (1.0 / MLSTM_V_DIM) + NORM_EPS)
        hh = hh * gout_ref[:, cols] * jax.nn.sigmoid(o_ref[rows, cols].astype(F32))
        h_ref[rows, cols] = hh.astype(h_ref.dtype)
        m_new = m_ts[h][L - 1:L, :]
        b_last = bs[h][L - 1:L, :]
        decays.append(jnp.exp(b_last + m_prevs[h] - m_new))
        w_in = jnp.exp(b_last - bs[h] + lis[h] - m_new)
        kws.append(jnp.where(halves[h % 2], k_pairs[h // 2].astype(F32) * w_in, 0.0))
        m_s[h:h + 1, :] = m_new

    for p in pairs:
        kw = jnp.concatenate([kws[2 * p], kws[2 * p + 1]], axis=0)
        vv = jnp.concatenate([v_heads[2 * p], v_heads[2 * p + 1]], axis=0)
        kw_t = kw.T.astype(BF16)
        decay = jnp.where(sub < MLSTM_QK_DIM, decays[2 * p], decays[2 * p + 1])
        c_s[p] = decay * c_pairs[p] + _dot(kw_t, vv)
        n_s[p] = decay * n_pairs[p] + _dot(kw_t, ones)


def _mlstm_scan_kernel(q_ref, k_ref, v_ref, o_ref, gate_ref, gout_ref, c0_ref, n0_ref, m0_ref,
                       h_ref, c_ref, n_ref, m_ref, c_s, n_s, m_s, *, chunk, chunks):
    j = pl.program_id(1)

    @pl.when(j == 0)
    def _():
        c_s[...] = c0_ref[0]
        n_s[...] = n0_ref[0]
        m_s[...] = m0_ref[0]

    for ci in range(chunks):
        _mlstm_chunk(slice(ci * chunk, (ci + 1) * chunk), q_ref, k_ref, v_ref, o_ref, gate_ref,
                     gout_ref, h_ref, c_s, n_s, m_s)

    @pl.when(j == pl.num_programs(1) - 1)
    def _():
        c_ref[0] = c_s[...]
        n_ref[0] = n_s[...]
        m_ref[0] = m_s[...]


def _mlstm_scan(proj, gates, g_out, c0, n0, m0, batch, seq):
    L = min(MLSTM_CHUNK, seq)
    chunks = min(MLSTM_CHUNKS_PER_STEP, seq // L)
    R = L * chunks
    nchunk = seq // R
    d = D_MODEL
    qk = MLSTM_HEADS * MLSTM_QK_DIM

    def row(b, j):
        return b * nchunk + j

    return pl.pallas_call(
        functools.partial(_mlstm_scan_kernel, chunk=L, chunks=chunks),
        grid=(batch, nchunk),
        in_specs=[
            pl.BlockSpec((R, qk), lambda b, j: (row(b, j), 0)),
            pl.BlockSpec((R, qk), lambda b, j: (row(b, j), 1)),
            pl.BlockSpec((R, d), lambda b, j: (row(b, j), 1)),
            pl.BlockSpec((R, d), lambda b, j: (row(b, j), 2)),
            pl.BlockSpec((R, LANES), lambda b, j: (row(b, j), 0)),
            _resident((1, d)),
            pl.BlockSpec((1, MLSTM_PAIRS, LANES, LANES), lambda b, j: (b, 0, 0, 0)),
            pl.BlockSpec((1, MLSTM_PAIRS, LANES, LANES), lambda b, j: (b, 0, 0, 0)),
            pl.BlockSpec((1, MLSTM_HEADS, LANES), lambda b, j: (b, 0, 0)),
        ],
        out_specs=[
            pl.BlockSpec((R, d), lambda b, j: (row(b, j), 0)),
            pl.BlockSpec((1, MLSTM_PAIRS, LANES, LANES), lambda b, j: (b, 0, 0, 0)),
            pl.BlockSpec((1, MLSTM_PAIRS, LANES, LANES), lambda b, j: (b, 0, 0, 0)),
            pl.BlockSpec((1, MLSTM_HEADS, LANES), lambda b, j: (b, 0, 0)),
        ],
        out_shape=[
            jax.ShapeDtypeStruct((batch * seq, d), BF16),
            jax.ShapeDtypeStruct((batch, MLSTM_PAIRS, LANES, LANES), F32),
            jax.ShapeDtypeStruct((batch, MLSTM_PAIRS, LANES, LANES), F32),
            jax.ShapeDtypeStruct((batch, MLSTM_HEADS, LANES), F32),
        ],
        scratch_shapes=[pltpu.VMEM((MLSTM_PAIRS, LANES, LANES), F32),
                        pltpu.VMEM((MLSTM_PAIRS, LANES, LANES), F32),
                        pltpu.VMEM((MLSTM_HEADS, LANES), F32)],
        compiler_params=_params(2),
        name="mlstm_scan",
    )(proj, proj, proj, proj, gates, g_out, c0, n0, m0)


def _ffn_kernel(x_ref, y_ref, wo_ref, g_ref, wu_ref, wc_ref, bc_ref, wd_ref, prev_ref, gf_ref,
                o_ref, tail_ref, xn_s, acc_s, carry_s, a_s, v_s, *, seq, tm, final_norm):
    sub_len = min(seq, tm)
    n_sub = tm // sub_len
    tiles_per_seq = max(1, seq // tm)
    i = pl.program_id(0)
    starts_seq = (i % tiles_per_seq) == 0

    x = x_ref[...] + _dot(y_ref[...], wo_ref[...])
    xn_s[...] = _rms(x, g_ref[...]).astype(BF16)
    acc_s[...] = x
    row = lax.broadcasted_iota(jnp.int32, (sub_len, 1), 0)

    if n_sub == 1:
        @pl.when(starts_seq)
        def _():
            carry_s[...] = prev_ref[0]

    def up(k):
        c0, n = FFN_SPLITS[k]
        xn = xn_s[...]
        a_s[:, c0:c0 + n] = _dot(xn, wu_ref[:, c0:c0 + n])
        v_s[:, c0:c0 + n] = _dot(xn, wu_ref[:, D_FF + c0:D_FF + c0 + n])

    def down(k):
        c0, n = FFN_SPLITS[k]
        cols = slice(c0, c0 + n)
        w = wc_ref[:, cols]
        b = bc_ref[:, cols]
        hs = []
        for s in range(n_sub):
            rows = slice(s * sub_len, (s + 1) * sub_len)
            a = a_s[rows, cols]
            prev = carry_s[:, cols] if n_sub == 1 else prev_ref[s, :, cols]
            p0, p1 = prev[0:1], prev[1:2]
            am1 = jnp.where(row == 0, p1, pltpu.roll(a, 1, 0))
            am2 = jnp.where(row == 0, p0, jnp.where(row == 1, p1, pltpu.roll(a, 2, 0)))
            conv = ((b + w[0:1] * am2) + w[1:2] * am1) + w[2:3] * a
            hs.append((jax.nn.gelu(conv) * v_s[rows, cols]).astype(BF16))
            tail = a[sub_len - 2:sub_len]
            tail_ref[s, :, cols] = tail
            if n_sub == 1:
                carry_s[:, cols] = tail
        h = hs[0] if n_sub == 1 else jnp.concatenate(hs, axis=0)
        acc_s[...] += _dot(h, wd_ref[cols, :])

    up(0)
    for k in range(len(FFN_SPLITS)):
        if k + 1 < len(FFN_SPLITS):
            up(k + 1)
        down(k)
    out = acc_s[...]
    o_ref[...] = _rms(out, gf_ref[...]) if final_norm else out


def _ffn(x, y, wo, g, w_up, wc, bc, w_dn, prev, g_final, layer, seq, final_norm):
    rows, d = x.shape
    tm = ROW_TILE
    batch = rows // seq
    if seq >= tm:
        seqs_per_tile = 1
        tiles_per_seq = seq // tm
        seq_block = lambda i: (i // tiles_per_seq, 0, 0)
    else:
        seqs_per_tile = tm // seq
        seq_block = lambda i: (i, 0, 0)
    state_spec = pl.BlockSpec((seqs_per_tile, 2, D_FF), seq_block)
    up_spec = pl.BlockSpec((None, d, 2 * D_FF), lambda i: (layer, 0, 0), pipeline_mode=pl.Buffered(1))
    down_spec = pl.BlockSpec((None, D_FF, d), lambda i: (layer, 0, 0), pipeline_mode=pl.Buffered(1))
    return pl.pallas_call(
        functools.partial(_ffn_kernel, seq=seq, tm=tm, final_norm=final_norm),
        grid=(rows // tm,),
        in_specs=[pl.BlockSpec((tm, d), lambda i: (i, 0)), pl.BlockSpec((tm, d), lambda i: (i, 0)),
                  _resident((d, d)), _resident((1, d)), up_spec,
                  _resident((3, D_FF)), _resident((1, D_FF)), down_spec, state_spec,
                  _resident((1, d))],
        out_specs=[pl.BlockSpec((tm, d), lambda i: (i, 0)), state_spec],
        out_shape=[jax.ShapeDtypeStruct((rows, d), F32),
                   jax.ShapeDtypeStruct((batch, 2, D_FF), F32)],
        scratch_shapes=[pltpu.VMEM((tm, d), BF16), pltpu.VMEM((tm, d), F32),
                        pltpu.VMEM((2, D_FF), F32),
                        pltpu.VMEM((tm, D_FF), F32), pltpu.VMEM((tm, D_FF), F32)],
        compiler_params=_params(1),
        name="conv_ffn",
    )(x, y, wo, g, w_up, wc, bc, w_dn, prev, g_final)


def _later_matrix(bk):
    j_idx = lax.broadcasted_iota(jnp.int32, (bk, bk), 0)
    s_idx = lax.broadcasted_iota(jnp.int32, (bk, bk), 1)
    return jnp.where(j_idx > s_idx, 1.0, 0.0).astype(BF16)


def _sb_visit(qs, kbs, vbs, state, later, strict_lower):
    heads = range(len(qs))
    zs = [_dot_nt(qs[h], kbs[h]) for h in heads]
    log_betas, log_keeps = [], []
    for h in heads:
        z = (zs[h] * (SB_HEAD_DIM ** -0.5)).astype(BF16)
        log_beta = _log_sigmoid(z)
        log_keep = log_beta - z
        if strict_lower is not None:
            log_keep = jnp.where(strict_lower, log_keep, jnp.zeros_like(log_keep))
        log_betas.append(log_beta)
        log_keeps.append(log_keep)
    afters = [_dot(log_keeps[h], later) for h in heads]
    weights, runs = [], []
    for h in heads:
        a = jnp.exp(log_betas[h].astype(F32) + afters[h] + state[h][0])
        if strict_lower is not None:
            a = jnp.where(strict_lower, a, 0.0)
        weights.append(a.astype(BF16))
        runs.append(state[h][0] + afters[h][:, 0:1] + log_keeps[h][:, 0:1].astype(F32))
    return tuple((runs[h], state[h][1] + _dot(weights[h], vbs[h])) for h in heads)


def _sb_alive(state):
    top = state[0][0]
    for run, _ in state[1:]:
        top = jnp.maximum(top, run)
    return jnp.max(top) > SB_DEAD_LOG


def _sb_prompt_kernel(q_ref, k_ref, v_ref, o_ref, *, blk, heads):
    i = pl.program_id(2)
    dh = SB_HEAD_DIM
    cols = [slice(h * dh, (h + 1) * dh) for h in range(heads)]
    qs = [q_ref[:, c] for c in cols]
    t_idx = lax.broadcasted_iota(jnp.int32, (blk, blk), 0)
    s_idx = lax.broadcasted_iota(jnp.int32, (blk, blk), 1)
    later = _later_matrix(blk)

    def visit(off, state, mask):
        return _sb_visit(qs, [k_ref[pl.ds(off, blk), c] for c in cols],
                         [v_ref[pl.ds(off, blk), c] for c in cols], state, later, mask)

    state = tuple((jnp.zeros((blk, 1), F32), jnp.zeros((blk, dh), F32)) for _ in range(heads))
    state = visit(pl.multiple_of(i * blk, blk), state, s_idx < t_idx)

    def cond(carry):
        n, alive, _ = carry
        return jnp.logical_and(n < i, alive)

    def body(carry):
        n, _, state = carry
        state = visit(pl.multiple_of((i - 1 - n) * blk, blk), state, None)
        return n + 1, _sb_alive(state), state

    _, _, state = lax.while_loop(cond, body, (jnp.int32(0), _sb_alive(state), state))
    for h in range(heads):
        o_ref[:, cols[h]] = state[h][1].astype(o_ref.dtype)


def _sb_prompt(q, kv, batch, seq):
    blk = SB_BLOCK
    nq = seq // blk
    heads = SB_HEADS_PER_STEP
    groups = SB_HEADS // heads
    width = heads * SB_HEAD_DIM
    return pl.pallas_call(
        functools.partial(_sb_prompt_kernel, blk=blk, heads=heads),
        grid=(batch, groups, nq),
        in_specs=[pl.BlockSpec((blk, width), lambda b, g, i: (b * nq + i, g)),
                  pl.BlockSpec((seq, width), lambda b, g, i: (b, g)),
                  pl.BlockSpec((seq, width), lambda b, g, i: (b, groups + g))],
        out_specs=pl.BlockSpec((blk, width), lambda b, g, i: (b * nq + i, g)),
        out_shape=jax.ShapeDtypeStruct((batch * seq, D_MODEL), BF16),
        compiler_params=_params(3),
        name="sb_attention_prompt",
    )(q, kv, kv)


def _sb_cached_kernel(q_ref, kv_ref, kc_hbm, vc_hbm, o_ref, kbuf, vbuf, sems, *, seq, blk, n_blocks):
    b = pl.program_id(0)
    dh = SB_HEAD_DIM
    heads = SB_HEADS
    rows = blk * heads
    cols = [slice(h * dh, (h + 1) * dh) for h in range(heads)]
    qs = [q_ref[:, c] for c in cols]

    def copies(n, slot):
        off = pl.multiple_of((n_blocks - 1 - n) * rows, rows)
        return (pltpu.make_async_copy(kc_hbm.at[b, pl.ds(off, rows), :], kbuf.at[slot], sems.at[0, slot]),
                pltpu.make_async_copy(vc_hbm.at[b, pl.ds(off, rows), :], vbuf.at[slot], sems.at[1, slot]))

    def start(n, slot):
        for c in copies(n, slot):
            c.start()

    def wait(n, slot):
        for c in copies(n, slot):
            c.wait()

    start(0, 0)

    t_idx = lax.broadcasted_iota(jnp.int32, (seq, seq), 0)
    s_idx = lax.broadcasted_iota(jnp.int32, (seq, seq), 1)
    state = tuple((jnp.zeros((seq, 1), F32), jnp.zeros((seq, dh), F32)) for _ in range(heads))
    state = _sb_visit(qs, [kv_ref[:, c] for c in cols],
                      [kv_ref[:, D_MODEL + h * dh:D_MODEL + (h + 1) * dh] for h in range(heads)],
                      state, _later_matrix(seq), s_idx < t_idx)
    later = _later_matrix(blk)

    def cond(carry):
        n, alive, _ = carry
        return jnp.logical_and(n < n_blocks, alive)

    def body(carry):
        n, _, state = carry
        slot = n % 2
        wait(n, slot)

        @pl.when(n + 1 < n_blocks)
        def _():
            start(n + 1, 1 - slot)

        state = _sb_visit(
            qs, [kbuf[slot, pl.ds(h, blk, stride=heads), :].astype(BF16) for h in range(heads)],
            [vbuf[slot, pl.ds(h, blk, stride=heads), :].astype(BF16) for h in range(heads)],
            state, later, None)
        return n + 1, _sb_alive(state), state

    n_done, _, state = lax.while_loop(cond, body, (jnp.int32(0), _sb_alive(state), state))

    @pl.when(n_done < n_blocks)
    def _():
        wait(n_done, n_done % 2)

    for h in range(heads):
        o_ref[:, cols[h]] = state[h][1].astype(o_ref.dtype)


def _sb_cached(q, kv, cache_k, cache_v, batch, seq):
    dh = SB_HEAD_DIM
    blk = SB_BLOCK
    rows = blk * SB_HEADS
    n_blocks = cache_k.shape[1] // rows
    return pl.pallas_call(
        functools.partial(_sb_cached_kernel, seq=seq, blk=blk, n_blocks=n_blocks),
        grid=(batch,),
        in_specs=[pl.BlockSpec((seq, D_MODEL), lambda b: (b, 0)),
                  pl.BlockSpec((seq, 2 * D_MODEL), lambda b: (b, 0)),
                  pl.BlockSpec(memory_space=pl.ANY),
                  pl.BlockSpec(memory_space=pl.ANY)],
        out_specs=pl.BlockSpec((seq, D_MODEL), lambda b: (b, 0)),
        out_shape=jax.ShapeDtypeStruct((batch * seq, D_MODEL), BF16),
        scratch_shapes=[pltpu.VMEM((2, rows, dh), F32), pltpu.VMEM((2, rows, dh), F32),
                        pltpu.SemaphoreType.DMA((2, 2))],
        compiler_params=_params(1),
        name="sb_attention_cached",
    )(q, kv, cache_k, cache_v)


def _prep_weights(g_mix_norm, w_mlstm_in, b_mlstm_gates, g_mlstm_out, w_mlstm_out,
                  g_kv_norm, w_kv, w_sb_q, w_sb_o,
                  g_ffn_norm, w_ffn_up, w_ffn_conv, b_ffn_conv, w_ffn_down, g_final):
    main = 2 * MLSTM_HEADS * MLSTM_QK_DIM + MLSTM_HEADS * MLSTM_V_DIM + D_MODEL
    n_gate = 2 * MLSTM_HEADS
    pad = LANES - n_gate
    return dict(
        g_mix=g_mix_norm.reshape(DEPTH, 1, D_MODEL),
        w_in=w_mlstm_in[:, :, :main].astype(BF16),
        w_gate=jnp.pad(w_mlstm_in[:, :, main:], ((0, 0), (0, 0), (0, pad))).astype(BF16),
        b_gate=jnp.pad(b_mlstm_gates, ((0, 0), (0, pad))).reshape(N_A_LAYERS, 1, LANES),
        g_out=g_mlstm_out.reshape(N_A_LAYERS, 1, D_MODEL),
        w_out=w_mlstm_out.astype(BF16),
        g_kv=g_kv_norm.reshape(1, D_MODEL),
        w_kv=w_kv.astype(BF16),
        w_q=w_sb_q.astype(BF16),
        w_o=w_sb_o.astype(BF16),
        g_ffn=g_ffn_norm.reshape(DEPTH, 1, D_MODEL),
        w_up=w_ffn_up.astype(BF16),
        w_c=w_ffn_conv,
        b_c=b_ffn_conv.reshape(DEPTH, 1, D_FF),
        w_d=w_ffn_down.astype(BF16),
        g_final=g_final.reshape(1, D_MODEL),
    )


def _trunk(x, c0, n0, m0, conv0, cache, w, batch, seq):
    new_c, new_n, new_m, new_conv = [], [], [], []
    kv_bf = k_new = v_new = None
    for l in range(DEPTH):
        if l < N_A_LAYERS:
            proj, gates = _mlstm_in(x, w["g_mix"][l], w["w_in"][l], w["w_gate"][l], w["b_gate"][l])
            mixed, c, n, m = _mlstm_scan(
                proj, gates, w["g_out"][l],
                c0[l].reshape(batch, MLSTM_PAIRS, LANES, LANES),
                jnp.broadcast_to(n0[l].reshape(batch, MLSTM_PAIRS, LANES, 1),
                                 (batch, MLSTM_PAIRS, LANES, LANES)),
                jnp.broadcast_to(m0[l].reshape(batch, MLSTM_HEADS, 1), (batch, MLSTM_HEADS, LANES)),
                batch, seq)
            w_mix = w["w_out"][l]
            new_c.append(c.reshape(batch, MLSTM_HEADS, MLSTM_QK_DIM, MLSTM_V_DIM))
            new_n.append(n[..., 0].reshape(batch, MLSTM_HEADS, MLSTM_QK_DIM))
            new_m.append(m[..., 0])
        else:
            j = l - N_A_LAYERS
            (q,) = _norm_matmul(x, w["g_mix"][l], w["w_q"][j], [BF16], "sb_q_proj")
            if cache is None:
                mixed = _sb_prompt(q, kv_bf, batch, seq)
            else:
                mixed = _sb_cached(q, kv_bf, cache[0], cache[1], batch, seq)
            w_mix = w["w_o"][j]
        x, tail = _ffn(x, mixed, w_mix, w["g_ffn"][l], w["w_up"], w["w_c"][l], w["b_c"][l],
                       w["w_d"], conv0[l], w["g_final"], l, seq, l == DEPTH - 1)
        new_conv.append(tail)
        if l == N_A_LAYERS - 1:
            k_rows, v_rows, kv_bf = _kv_proj(x, w["g_kv"], w["w_kv"])
            k_new = k_rows.reshape(batch, seq, SB_HEADS, SB_HEAD_DIM)
            v_new = v_rows.reshape(batch, seq, SB_HEADS, SB_HEAD_DIM)
    y = x.reshape(batch, seq, D_MODEL)
    return (y, jnp.stack(new_c), jnp.stack(new_n), jnp.stack(new_m), jnp.stack(new_conv),
            k_new, v_new)


def kernel(x_prompt, x_sample, cache_k, cache_v, state_mlstm_c, state_mlstm_n, state_mlstm_m,
           state_ffn_conv, g_mix_norm, w_mlstm_in, b_mlstm_gates, g_mlstm_out, w_mlstm_out,
           g_kv_norm, w_kv, w_sb_q, w_sb_o, g_ffn_norm, w_ffn_up, w_ffn_conv, b_ffn_conv,
           w_ffn_down, g_final):
    w = _prep_weights(g_mix_norm, w_mlstm_in, b_mlstm_gates, g_mlstm_out, w_mlstm_out,
                      g_kv_norm, w_kv, w_sb_q, w_sb_o,
                      g_ffn_norm, w_ffn_up, w_ffn_conv, b_ffn_conv, w_ffn_down, g_final)
    bp, tp, d = x_prompt.shape
    bs, ts, _ = x_sample.shape
    past = cache_k.shape[1]

    p_c0 = jnp.zeros((N_A_LAYERS, bp, MLSTM_HEADS, MLSTM_QK_DIM, MLSTM_V_DIM), F32)
    p_n0 = jnp.zeros((N_A_LAYERS, bp, MLSTM_HEADS, MLSTM_QK_DIM), F32)
    p_m0 = jnp.full((N_A_LAYERS, bp, MLSTM_HEADS), NEG_BIG, F32)
    p_conv0 = jnp.zeros((DEPTH, bp, 2, D_FF), F32)
    p_out = _trunk(x_prompt.reshape(bp * tp, d), p_c0, p_n0, p_m0, p_conv0, None, w, bp, tp)

    cache = (cache_k.reshape(bs, past * SB_HEADS, SB_HEAD_DIM),
             cache_v.reshape(bs, past * SB_HEADS, SB_HEAD_DIM))
    s_out = _trunk(x_sample.reshape(bs * ts, d), state_mlstm_c, state_mlstm_n, state_mlstm_m,
                   state_ffn_conv, cache, w, bs, ts)
    return (p_out[0], s_out[0]) + p_out[1:] + s_out[1:]
```

```python
import functools

import jax
import jax.numpy as jnp
from jax import lax
from jax.experimental import pallas as pl
from jax.experimental.pallas import tpu as pltpu

F32 = jnp.float32
BF16 = jnp.bfloat16

D_MODEL = 1024
DEPTH = 4
N_A_LAYERS = 2
MLSTM_HEADS = 8
MLSTM_QK_DIM = 64
MLSTM_V_DIM = 128
MLSTM_PAIRS = MLSTM_HEADS // 2
GATE_SOFTCAP = 15.0
SB_HEADS = 8
SB_HEAD_DIM = 128
D_FF = 2816
NORM_EPS = 1e-6
NEG_BIG = -1e30

LANES = 128
ROW_TILE = 512
FFN_SPLITS = ((0, 1024), (1024, 1024), (2048, 768))
MLSTM_CHUNK = 128
MLSTM_CHUNKS_PER_STEP = 4
SB_BLOCK = 256
SB_HEADS_PER_STEP = 8
SB_DEAD_LOG = -105.0
VMEM_LIMIT = 56 * 1024 * 1024


def _params(n_axes):
    return pltpu.CompilerParams(dimension_semantics=("arbitrary",) * n_axes,
                                vmem_limit_bytes=VMEM_LIMIT)


def _resident(shape):
    zeros = (0,) * len(shape)
    return pl.BlockSpec(shape, lambda *_: zeros, pipeline_mode=pl.Buffered(1))


def _rms(x, g):
    return x * lax.rsqrt(jnp.mean(x * x, axis=-1, keepdims=True) + NORM_EPS) * g


def _dot(a, b):
    return jnp.dot(a, b, preferred_element_type=F32)


def _dot_nt(a, b):
    return lax.dot_general(a, b, (((1,), (1,)), ((), ())), preferred_element_type=F32)


def _split3(x):
    hi = x.astype(BF16)
    r = x - hi.astype(F32)
    mid = r.astype(BF16)
    lo = (r - mid.astype(F32)).astype(BF16)
    return hi, mid, lo


def _log_sigmoid(z):
    return jnp.minimum(z, 0.0) - jnp.log(1.0 + jnp.exp(-jnp.abs(z)))


def _norm_matmul_kernel(x_ref, g_ref, w_ref, *o_refs, col_chunk):
    xn = _rms(x_ref[...], g_ref[...]).astype(BF16)
    n = w_ref.shape[1]
    for c0 in range(0, n, col_chunk):
        y = _dot(xn, w_ref[:, c0:c0 + col_chunk])
        for o_ref in o_refs:
            o_ref[:, c0:c0 + col_chunk] = y.astype(o_ref.dtype)


def _norm_matmul(x, g, w, out_dtypes, name):
    rows, d = x.shape
    n = w.shape[1]
    tm = ROW_TILE
    return pl.pallas_call(
        functools.partial(_norm_matmul_kernel, col_chunk=512),
        grid=(rows // tm,),
        in_specs=[pl.BlockSpec((tm, d), lambda i: (i, 0)), _resident((1, d)), _resident((d, n))],
        out_specs=[pl.BlockSpec((tm, n), lambda i: (i, 0)) for _ in out_dtypes],
        out_shape=[jax.ShapeDtypeStruct((rows, n), dt) for dt in out_dtypes],
        compiler_params=_params(1),
        name=name,
    )(x, g, w)


def _kv_proj_kernel(x_ref, g_ref, w_ref, k_ref, v_ref, kv_ref):
    xn = _rms(x_ref[...], g_ref[...]).astype(BF16)
    tm = x_ref.shape[0]
    group = 4
    for dst, base in ((k_ref, 0), (v_ref, D_MODEL)):
        for h0 in range(0, SB_HEADS, group):
            c0 = base + h0 * SB_HEAD_DIM
            y = _dot(xn, w_ref[:, c0:c0 + group * SB_HEAD_DIM])
            kv_ref[:, c0:c0 + group * SB_HEAD_DIM] = y.astype(kv_ref.dtype)
            for h in range(group):
                dst[pl.ds(h0 + h, tm, stride=SB_HEADS), :] = y[:, h * SB_HEAD_DIM:(h + 1) * SB_HEAD_DIM]


def _kv_proj(x, g, w):
    rows, d = x.shape
    tm = ROW_TILE
    head_rows = pl.BlockSpec((tm * SB_HEADS, SB_HEAD_DIM), lambda i: (i, 0))
    return pl.pallas_call(
        _kv_proj_kernel,
        grid=(rows // tm,),
        in_specs=[pl.BlockSpec((tm, d), lambda i: (i, 0)), _resident((1, d)), _resident((d, 2 * d))],
        out_specs=[head_rows, head_rows, pl.BlockSpec((tm, 2 * d), lambda i: (i, 0))],
        out_shape=[jax.ShapeDtypeStruct((rows * SB_HEADS, SB_HEAD_DIM), F32),
                   jax.ShapeDtypeStruct((rows * SB_HEADS, SB_HEAD_DIM), F32),
                   jax.ShapeDtypeStruct((rows, 2 * d), BF16)],
        compiler_params=_params(1),
        name="kv_proj",
    )(x, g, w)


def _mlstm_in_kernel(x_ref, g_ref, w_ref, wg_ref, bg_ref, o_ref, gate_ref, *, col_chunk):
    xn = _rms(x_ref[...], g_ref[...]).astype(BF16)
    n = w_ref.shape[1]
    for c0 in range(0, n, col_chunk):
        o_ref[:, c0:c0 + col_chunk] = _dot(xn, w_ref[:, c0:c0 + col_chunk]).astype(o_ref.dtype)
    pre = _dot(xn, wg_ref[...]) + bg_ref[...]
    capped = GATE_SOFTCAP * jnp.tanh(pre / GATE_SOFTCAP)
    lane = lax.broadcasted_iota(jnp.int32, capped.shape, 1)
    gate_ref[...] = jnp.where(lane < MLSTM_HEADS, capped, _log_sigmoid(capped))


def _mlstm_in(x, g, w, wg, bg):
    rows, d = x.shape
    n = w.shape[1]
    tm = ROW_TILE
    return pl.pallas_call(
        functools.partial(_mlstm_in_kernel, col_chunk=512),
        grid=(rows // tm,),
        in_specs=[pl.BlockSpec((tm, d), lambda i: (i, 0)), _resident((1, d)), _resident((d, n)),
                  _resident((d, LANES)), _resident((1, LANES))],
        out_specs=[pl.BlockSpec((tm, n), lambda i: (i, 0)),
                   pl.BlockSpec((tm, LANES), lambda i: (i, 0))],
        out_shape=[jax.ShapeDtypeStruct((rows, n), BF16),
                   jax.ShapeDtypeStruct((rows, LANES), F32)],
        compiler_params=_params(1),
        name="mlstm_in_proj",
    )(x, g, w, wg, bg)


def _transpose_rows(a):
    rows = a.shape[0]
    if rows < LANES:
        a = jnp.concatenate([a, jnp.zeros((LANES - rows, LANES), F32)], axis=0)
        return a.T[:, :rows]
    return a.T


def _mlstm_chunk(rows, q_ref, k_ref, v_ref, o_ref, gate_ref, gout_ref, h_ref, c_s, n_s, m_s):
    L = rows.stop - rows.start
    gates = gate_ref[rows, :]
    t_idx = lax.broadcasted_iota(jnp.int32, (L, L), 0)
    s_idx = lax.broadcasted_iota(jnp.int32, (L, L), 1)
    causal = s_idx <= t_idx
    tri = jnp.where(causal, 1.0, 0.0).astype(BF16)
    g_hi, g_mid, g_lo = _split3(gates)
    cum = _dot(tri, g_hi) + _dot(tri, g_mid) + _dot(tri, g_lo)
    gates_t = _transpose_rows(gates)
    cum_t = _transpose_rows(cum)
    lane = lax.broadcasted_iota(jnp.int32, (L, LANES), 1)
    sub = lax.broadcasted_iota(jnp.int32, (LANES, LANES), 0)
    ones = jnp.ones((2 * L, LANES), BF16)
    scale = MLSTM_QK_DIM ** -0.5

    def row_sum(x):
        hi = x.astype(BF16)
        lo = (x - hi.astype(F32)).astype(BF16)
        width = x.shape[1]
        return _dot(hi, ones[:width]) + _dot(lo, ones[:width])

    heads = range(MLSTM_HEADS)
    pairs = range(MLSTM_PAIRS)
    halves = [(lane // MLSTM_QK_DIM) == e for e in range(2)]
    k_pairs = [k_ref[rows, p * LANES:(p + 1) * LANES] for p in pairs]
    c_pairs = [c_s[p] for p in pairs]
    n_pairs = [n_s[p] for p in pairs]
    v_heads = [v_ref[rows, h * MLSTM_V_DIM:(h + 1) * MLSTM_V_DIM] for h in heads]
    m_prevs = [m_s[h:h + 1, :] for h in heads]

    qk, qc, qn = [], [], []
    for p in pairs:
        q_f32 = q_ref[rows, p * LANES:(p + 1) * LANES].astype(F32)
        c_bf = c_pairs[p].astype(BF16)
        n_bf = n_pairs[p].astype(BF16)
        for e in range(2):
            qm = jnp.where(halves[e], q_f32, 0.0).astype(BF16)
            qk.append(_dot_nt(qm, k_pairs[p]))
            qc.append(_dot(qm, c_bf))
            qn.append(_dot(qm, n_bf))

    lis, bs, m_ts, w_inters, ss = [], [], [], [], []
    for h in heads:
        li = jnp.broadcast_to(gates[:, h:h + 1], (L, LANES))
        b = jnp.broadcast_to(cum[:, MLSTM_HEADS + h:MLSTM_HEADS + h + 1], (L, LANES))
        li_row = gates_t[h:h + 1, :]
        b_row = cum_t[MLSTM_HEADS + h:MLSTM_HEADS + h + 1, :]
        log_d = jnp.where(causal, b[:, :L] + (li_row - b_row), NEG_BIG)
        m_inter = b + m_prevs[h]
        m_t = jnp.maximum(m_inter, jnp.max(log_d, axis=-1, keepdims=True))
        lis.append(li)
        bs.append(b)
        m_ts.append(m_t)
        w_inters.append(jnp.exp(m_inter - m_t))
        ss.append(qk[h] * scale * jnp.exp(log_d - m_t[:, :L]))

    sv = [_dot(ss[h].astype(BF16), v_heads[h]) for h in heads]
    s_sum = [row_sum(ss[h]) for h in heads]

    hs = []
    for h in heads:
        num = sv[h] + w_inters[h] * (qc[h] * scale)
        den = s_sum[h] + w_inters[h] * (qn[h] * scale)
        hs.append(num / jnp.maximum(jnp.abs(den), jnp.exp(-m_ts[h])))
    sq_sum = [row_sum(hs[h] * hs[h]) for h in heads]

    kws, decays = [], []
    for h in heads:
        cols = slice(h * MLSTM_V_DIM, (h + 1) * MLSTM_V_DIM)
        hh = hs[h] * lax.rsqrt(sq_sum[h] * (1.0 / MLSTM_V_DIM) + NORM_EPS)
        hh = hh * gout_ref[:, cols] * jax.nn.sigmoid(o_ref[rows, cols].astype(F32))
        h_ref[rows, cols] = hh.astype(h_ref.dtype)
        m_new = m_ts[h][L - 1:L, :]
        b_last = bs[h][L - 1:L, :]
        decays.append(jnp.exp(b_last + m_prevs[h] - m_new))
        w_in = jnp.exp(b_last - bs[h] + lis[h] - m_new)
        kws.append(jnp.where(halves[h % 2], k_pairs[h // 2].astype(F32) * w_in, 0.0))
        m_s[h:h + 1, :] = m_new

    for p in pairs:
        kw = jnp.concatenate([kws[2 * p], kws[2 * p + 1]], axis=0)
        vv = jnp.concatenate([v_heads[2 * p], v_heads[2 * p + 1]], axis=0)
        kw_t = kw.T.astype(BF16)
        decay = jnp.where(sub < MLSTM_QK_DIM, decays[2 * p], decays[2 * p + 1])
        c_s[p] = decay * c_pairs[p] + _dot(kw_t, vv)
        n_s[p] = decay * n_pairs[p] + _dot(kw_t, ones)


def _mlstm_scan_kernel(q_ref, k_ref, v_ref, o_ref, gate_ref, gout_ref, c0_ref, n0_ref, m0_ref,
                       h_ref, c_ref, n_ref, m_ref, c_s, n_s, m_s, *, chunk, chunks):
    j = pl.program_id(1)

    @pl.when(j == 0)
    def _():
        c_s[...] = c0_ref[0]
        n_s[...] = n0_ref[0]
        m_s[...] = m0_ref[0]

    for ci in range(chunks):
        _mlstm_chunk(slice(ci * chunk, (ci + 1) * chunk), q_ref, k_ref, v_ref, o_ref, gate_ref,
                     gout_ref, h_ref, c_s, n_s, m_s)

    @pl.when(j == pl.num_programs(1) - 1)
    def _():
        c_ref[0] = c_s[...]
        n_ref[0] = n_s[...]
        m_ref[0] = m_s[...]


def _mlstm_scan(proj, gates, g_out, c0, n0, m0, batch, seq):
    L = min(MLSTM_CHUNK, seq)
    chunks = min(MLSTM_CHUNKS_PER_STEP, seq // L)
    R = L * chunks
    nchunk = seq // R
    d = D_MODEL
    qk = MLSTM_HEADS * MLSTM_QK_DIM

    def row(b, j):
        return b * nchunk + j

    return pl.pallas_call(
        functools.partial(_mlstm_scan_kernel, chunk=L, chunks=chunks),
        grid=(batch, nchunk),
        in_specs=[
            pl.BlockSpec((R, qk), lambda b, j: (row(b, j), 0)),
            pl.BlockSpec((R, qk), lambda b, j: (row(b, j), 1)),
            pl.BlockSpec((R, d), lambda b, j: (row(b, j), 1)),
            pl.BlockSpec((R, d), lambda b, j: (row(b, j), 2)),
            pl.BlockSpec((R, LANES), lambda b, j: (row(b, j), 0)),
            _resident((1, d)),
            pl.BlockSpec((1, MLSTM_PAIRS, LANES, LANES), lambda b, j: (b, 0, 0, 0)),
            pl.BlockSpec((1, MLSTM_PAIRS, LANES, LANES), lambda b, j: (b, 0, 0, 0)),
            pl.BlockSpec((1, MLSTM_HEADS, LANES), lambda b, j: (b, 0, 0)),
        ],
        out_specs=[
            pl.BlockSpec((R, d), lambda b, j: (row(b, j), 0)),
            pl.BlockSpec((1, MLSTM_PAIRS, LANES, LANES), lambda b, j: (b, 0, 0, 0)),
            pl.BlockSpec((1, MLSTM_PAIRS, LANES, LANES), lambda b, j: (b, 0, 0, 0)),
            pl.BlockSpec((1, MLSTM_HEADS, LANES), lambda b, j: (b, 0, 0)),
        ],
        out_shape=[
            jax.ShapeDtypeStruct((batch * seq, d), BF16),
            jax.ShapeDtypeStruct((batch, MLSTM_PAIRS, LANES, LANES), F32),
            jax.ShapeDtypeStruct((batch, MLSTM_PAIRS, LANES, LANES), F32),
            jax.ShapeDtypeStruct((batch, MLSTM_HEADS, LANES), F32),
        ],
        scratch_shapes=[pltpu.VMEM((MLSTM_PAIRS, LANES, LANES), F32),
                        pltpu.VMEM((MLSTM_PAIRS, LANES, LANES), F32),
                        pltpu.VMEM((MLSTM_HEADS, LANES), F32)],
        compiler_params=_params(2),
        name="mlstm_scan",
    )(proj, proj, proj, proj, gates, g_out, c0, n0, m0)


def _ffn_kernel(x_ref, y_ref, wo_ref, g_ref, wu_ref, wc_ref, bc_ref, wd_ref, prev_ref, gf_ref,
                o_ref, tail_ref, xn_s, acc_s, carry_s, a_s, v_s, *, seq, tm, final_norm):
    sub_len = min(seq, tm)
    n_sub = tm // sub_len
    tiles_per_seq = max(1, seq // tm)
    i = pl.program_id(0)
    starts_seq = (i % tiles_per_seq) == 0

    x = x_ref[...] + _dot(y_ref[...], wo_ref[...])
    xn_s[...] = _rms(x, g_ref[...]).astype(BF16)
    acc_s[...] = x
    row = lax.broadcasted_iota(jnp.int32, (sub_len, 1), 0)

    if n_sub == 1:
        @pl.when(starts_seq)
        def _():
            carry_s[...] = prev_ref[0]

    def up(k):
        c0, n = FFN_SPLITS[k]
        xn = xn_s[...]
        a_s[:, c0:c0 + n] = _dot(xn, wu_ref[:, c0:c0 + n])
        v_s[:, c0:c0 + n] = _dot(xn, wu_ref[:, D_FF + c0:D_FF + c0 + n])

    def down(k):
        c0, n = FFN_SPLITS[k]
        cols = slice(c0, c0 + n)
        w = wc_ref[:, cols]
        b = bc_ref[:, cols]
        hs = []
        for s in range(n_sub):
            rows = slice(s * sub_len, (s + 1) * sub_len)
            a = a_s[rows, cols]
            prev = carry_s[:, cols] if n_sub == 1 else prev_ref[s, :, cols]
            p0, p1 = prev[0:1], prev[1:2]
            am1 = jnp.where(row == 0, p1, pltpu.roll(a, 1, 0))
            am2 = jnp.where(row == 0, p0, jnp.where(row == 1, p1, pltpu.roll(a, 2, 0)))
            conv = ((b + w[0:1] * am2) + w[1:2] * am1) + w[2:3] * a
            hs.append((jax.nn.gelu(conv) * v_s[rows, cols]).astype(BF16))
            tail = a[sub_len - 2:sub_len]
            tail_ref[s, :, cols] = tail
            if n_sub == 1:
                carry_s[:, cols] = tail
        h = hs[0] if n_sub == 1 else jnp.concatenate(hs, axis=0)
        acc_s[...] += _dot(h, wd_ref[cols, :])

    up(0)
    for k in range(len(FFN_SPLITS)):
        if k + 1 < len(FFN_SPLITS):
            up(k + 1)
        down(k)
    out = acc_s[...]
    o_ref[...] = _rms(out, gf_ref[...]) if final_norm else out


def _ffn(x, y, wo, g, w_up, wc, bc, w_dn, prev, g_final, layer, seq, final_norm):
    rows, d = x.shape
    tm = ROW_TILE
    batch = rows // seq
    if seq >= tm:
        seqs_per_tile = 1
        tiles_per_seq = seq // tm
        seq_block = lambda i: (i // tiles_per_seq, 0, 0)
    else:
        seqs_per_tile = tm // seq
        seq_block = lambda i: (i, 0, 0)
    state_spec = pl.BlockSpec((seqs_per_tile, 2, D_FF), seq_block)
    up_spec = pl.BlockSpec((None, d, 2 * D_FF), lambda i: (layer, 0, 0), pipeline_mode=pl.Buffered(1))
    down_spec = pl.BlockSpec((None, D_FF, d), lambda i: (layer, 0, 0), pipeline_mode=pl.Buffered(1))
    return pl.pallas_call(
        functools.partial(_ffn_kernel, seq=seq, tm=tm, final_norm=final_norm),
        grid=(rows // tm,),
        in_specs=[pl.BlockSpec((tm, d), lambda i: (i, 0)), pl.BlockSpec((tm, d), lambda i: (i, 0)),
                  _resident((d, d)), _resident((1, d)), up_spec,
                  _resident((3, D_FF)), _resident((1, D_FF)), down_spec, state_spec,
                  _resident((1, d))],
        out_specs=[pl.BlockSpec((tm, d), lambda i: (i, 0)), state_spec],
        out_shape=[jax.ShapeDtypeStruct((rows, d), F32),
                   jax.ShapeDtypeStruct((batch, 2, D_FF), F32)],
        scratch_shapes=[pltpu.VMEM((tm, d), BF16), pltpu.VMEM((tm, d), F32),
                        pltpu.VMEM((2, D_FF), F32),
                        pltpu.VMEM((tm, D_FF), F32), pltpu.VMEM((tm, D_FF), F32)],
        compiler_params=_params(1),
        name="conv_ffn",
    )(x, y, wo, g, w_up, wc, bc, w_dn, prev, g_final)


def _later_matrix(bk):
    j_idx = lax.broadcasted_iota(jnp.int32, (bk, bk), 0)
    s_idx = lax.broadcasted_iota(jnp.int32, (bk, bk), 1)
    return jnp.where(j_idx > s_idx, 1.0, 0.0).astype(BF16)


def _sb_visit(qs, kbs, vbs, state, later, strict_lower):
    heads = range(len(qs))
    zs = [_dot_nt(qs[h], kbs[h]) for h in heads]
    log_betas, log_keeps = [], []
    for h in heads:
        z = (zs[h] * (SB_HEAD_DIM ** -0.5)).astype(BF16)
        log_beta = _log_sigmoid(z)
        log_keep = log_beta - z
        if strict_lower is not None:
            log_keep = jnp.where(strict_lower, log_keep, jnp.zeros_like(log_keep))
        log_betas.append(log_beta)
        log_keeps.append(log_keep)
    afters = [_dot(log_keeps[h], later) for h in heads]
    weights, runs = [], []
    for h in heads:
        a = jnp.exp(log_betas[h].astype(F32) + afters[h] + state[h][0])
        if strict_lower is not None:
            a = jnp.where(strict_lower, a, 0.0)
        weights.append(a.astype(BF16))
        runs.append(state[h][0] + afters[h][:, 0:1] + log_keeps[h][:, 0:1].astype(F32))
    return tuple((runs[h], state[h][1] + _dot(weights[h], vbs[h])) for h in heads)


def _sb_alive(state):
    top = state[0][0]
    for run, _ in state[1:]:
        top = jnp.maximum(top, run)
    return jnp.max(top) > SB_DEAD_LOG


def _sb_prompt_kernel(q_ref, k_ref, v_ref, o_ref, *, blk, heads):
    i = pl.program_id(2)
    dh = SB_HEAD_DIM
    cols = [slice(h * dh, (h + 1) * dh) for h in range(heads)]
    qs = [q_ref[:, c] for c in cols]
    t_idx = lax.broadcasted_iota(jnp.int32, (blk, blk), 0)
    s_idx = lax.broadcasted_iota(jnp.int32, (blk, blk), 1)
    later = _later_matrix(blk)

    def visit(off, state, mask):
        return _sb_visit(qs, [k_ref[pl.ds(off, blk), c] for c in cols],
                         [v_ref[pl.ds(off, blk), c] for c in cols], state, later, mask)

    state = tuple((jnp.zeros((blk, 1), F32), jnp.zeros((blk, dh), F32)) for _ in range(heads))
    state = visit(pl.multiple_of(i * blk, blk), state, s_idx < t_idx)

    def cond(carry):
        n, alive, _ = carry
        return jnp.logical_and(n < i, alive)

    def body(carry):
        n, _, state = carry
        state = visit(pl.multiple_of((i - 1 - n) * blk, blk), state, None)
        return n + 1, _sb_alive(state), state

    _, _, state = lax.while_loop(cond, body, (jnp.int32(0), _sb_alive(state), state))
    for h in range(heads):
        o_ref[:, cols[h]] = state[h][1].astype(o_ref.dtype)


def _sb_prompt(q, kv, batch, seq):
    blk = SB_BLOCK
    nq = seq // blk
    heads = SB_HEADS_PER_STEP
    groups = SB_HEADS // heads
    width = heads * SB_HEAD_DIM
    return pl.pallas_call(
        functools.partial(_sb_prompt_kernel, blk=blk, heads=heads),
        grid=(batch, groups, nq),
        in_specs=[pl.BlockSpec((blk, width), lambda b, g, i: (b * nq + i, g)),
                  pl.BlockSpec((seq, width), lambda b, g, i: (b, g)),
                  pl.BlockSpec((seq, width), lambda b, g, i: (b, groups + g))],
        out_specs=pl.BlockSpec((blk, width), lambda b, g, i: (b * nq + i, g)),
        out_shape=jax.ShapeDtypeStruct((batch * seq, D_MODEL), BF16),
        compiler_params=_params(3),
        name="sb_attention_prompt",
    )(q, kv, kv)


def _sb_cached_kernel(q_ref, kv_ref, kc_hbm, vc_hbm, o_ref, kbuf, vbuf, sems, *, seq, blk, n_blocks):
    b = pl.program_id(0)
    dh = SB_HEAD_DIM
    heads = SB_HEADS
    rows = blk * heads
    cols = [slice(h * dh, (h + 1) * dh) for h in range(heads)]
    qs = [q_ref[:, c] for c in cols]

    def copies(stream, n, slot):
        off = pl.multiple_of((n_blocks - 1 - n) * rows, rows)
        return (pltpu.make_async_copy(kc_hbm.at[stream, pl.ds(off, rows), :], kbuf.at[slot],
                                      sems.at[0, slot]),
                pltpu.make_async_copy(vc_hbm.at[stream, pl.ds(off, rows), :], vbuf.at[slot],
                                      sems.at[1, slot]))

    def start(stream, n, slot):
        for c in copies(stream, n, slot):
            c.start()

    def wait(n, slot):
        for c in copies(b, n, slot):
            c.wait()

    @pl.when(b == 0)
    def _():
        start(b, 0, 0)

    t_idx = lax.broadcasted_iota(jnp.int32, (seq, seq), 0)
    s_idx = lax.broadcasted_iota(jnp.int32, (seq, seq), 1)
    state = tuple((jnp.zeros((seq, 1), F32), jnp.zeros((seq, dh), F32)) for _ in range(heads))
    state = _sb_visit(qs, [kv_ref[:, c] for c in cols],
                      [kv_ref[:, D_MODEL + h * dh:D_MODEL + (h + 1) * dh] for h in range(heads)],
                      state, _later_matrix(seq), s_idx < t_idx)
    later = _later_matrix(blk)

    def cond(carry):
        n, alive, _ = carry
        return jnp.logical_and(n < n_blocks, alive)

    def body(carry):
        n, _, state = carry
        slot = n % 2
        wait(n, slot)

        @pl.when(jnp.logical_and(n >= 1, n + 1 < n_blocks))
        def _():
            start(b, n + 1, 1 - slot)

        state = _sb_visit(
            qs, [kbuf[slot, pl.ds(h, blk, stride=heads), :].astype(BF16) for h in range(heads)],
            [vbuf[slot, pl.ds(h, blk, stride=heads), :].astype(BF16) for h in range(heads)],
            state, later, None)
        alive = _sb_alive(state)

        if n_blocks > 1:
            @pl.when(jnp.logical_and(n == 0, alive))
            def _():
                start(b, 1, 1)

        return n + 1, alive, state

    n_done, _, state = lax.while_loop(cond, body, (jnp.int32(0), _sb_alive(state), state))

    @pl.when(jnp.logical_or(n_done == 0, jnp.logical_and(n_done >= 2, n_done < n_blocks)))
    def _():
        wait(n_done, n_done % 2)

    @pl.when(b + 1 < pl.num_programs(0))
    def _():
        start(b + 1, 0, 0)

    for h in range(heads):
        o_ref[:, cols[h]] = state[h][1].astype(o_ref.dtype)


def _sb_cached(q, kv, cache_k, cache_v, batch, seq):
    dh = SB_HEAD_DIM
    blk = SB_BLOCK
    rows = blk * SB_HEADS
    n_blocks = cache_k.shape[1] // rows
    return pl.pallas_call(
        functools.partial(_sb_cached_kernel, seq=seq, blk=blk, n_blocks=n_blocks),
        grid=(batch,),
        in_specs=[pl.BlockSpec((seq, D_MODEL), lambda b: (b, 0)),
                  pl.BlockSpec((seq, 2 * D_MODEL), lambda b: (b, 0)),
                  pl.BlockSpec(memory_space=pl.ANY),
                  pl.BlockSpec(memory_space=pl.ANY)],
        out_specs=pl.BlockSpec((seq, D_MODEL), lambda b: (b, 0)),
        out_shape=jax.ShapeDtypeStruct((batch * seq, D_MODEL), BF16),
        scratch_shapes=[pltpu.VMEM((2, rows, dh), F32), pltpu.VMEM((2, rows, dh), F32),
                        pltpu.SemaphoreType.DMA((2, 2))],
        compiler_params=_params(1),
        name="sb_attention_cached",
    )(q, kv, cache_k, cache_v)


def _prep_weights(g_mix_norm, w_mlstm_in, b_mlstm_gates, g_mlstm_out, w_mlstm_out,
                  g_kv_norm, w_kv, w_sb_q, w_sb_o,
                  g_ffn_norm, w_ffn_up, w_ffn_conv, b_ffn_conv, w_ffn_down, g_final):
    main = 2 * MLSTM_HEADS * MLSTM_QK_DIM + MLSTM_HEADS * MLSTM_V_DIM + D_MODEL
    n_gate = 2 * MLSTM_HEADS
    pad = LANES - n_gate
    return dict(
        g_mix=g_mix_norm.reshape(DEPTH, 1, D_MODEL),
        w_in=w_mlstm_in[:, :, :main].astype(BF16),
        w_gate=jnp.pad(w_mlstm_in[:, :, main:], ((0, 0), (0, 0), (0, pad))).astype(BF16),
        b_gate=jnp.pad(b_mlstm_gates, ((0, 0), (0, pad))).reshape(N_A_LAYERS, 1, LANES),
        g_out=g_mlstm_out.reshape(N_A_LAYERS, 1, D_MODEL),
        w_out=w_mlstm_out.astype(BF16),
        g_kv=g_kv_norm.reshape(1, D_MODEL),
        w_kv=w_kv.astype(BF16),
        w_q=w_sb_q.astype(BF16),
        w_o=w_sb_o.astype(BF16),
        g_ffn=g_ffn_norm.reshape(DEPTH, 1, D_MODEL),
        w_up=w_ffn_up.astype(BF16),
        w_c=w_ffn_conv,
        b_c=b_ffn_conv.reshape(DEPTH, 1, D_FF),
        w_d=w_ffn_down.astype(BF16),
        g_final=g_final.reshape(1, D_MODEL),
    )


def _trunk(x, c0, n0, m0, conv0, cache, w, batch, seq):
    new_c, new_n, new_m, new_conv = [], [], [], []
    kv_bf = k_new = v_new = None
    for l in range(DEPTH):
        if l < N_A_LAYERS:
            proj, gates = _mlstm_in(x, w["g_mix"][l], w["w_in"][l], w["w_gate"][l], w["b_gate"][l])
            mixed, c, n, m = _mlstm_scan(
                proj, gates, w["g_out"][l],
                c0[l].reshape(batch, MLSTM_PAIRS, LANES, LANES),
                jnp.broadcast_to(n0[l].reshape(batch, MLSTM_PAIRS, LANES, 1),
                                 (batch, MLSTM_PAIRS, LANES, LANES)),
                jnp.broadcast_to(m0[l].reshape(batch, MLSTM_HEADS, 1), (batch, MLSTM_HEADS, LANES)),
                batch, seq)
            w_mix = w["w_out"][l]
            new_c.append(c.reshape(batch, MLSTM_HEADS, MLSTM_QK_DIM, MLSTM_V_DIM))
            new_n.append(n[..., 0].reshape(batch, MLSTM_HEADS, MLSTM_QK_DIM))
            new_m.append(m[..., 0])
        else:
            j = l - N_A_LAYERS
            (q,) = _norm_matmul(x, w["g_mix"][l], w["w_q"][j], [BF16], "sb_q_proj")
            if cache is None:
                mixed = _sb_prompt(q, kv_bf, batch, seq)
            else:
                mixed = _sb_cached(q, kv_bf, cache[0], cache[1], batch, seq)
            w_mix = w["w_o"][j]
        x, tail = _ffn(x, mixed, w_mix, w["g_ffn"][l], w["w_up"], w["w_c"][l], w["b_c"][l],
                       w["w_d"], conv0[l], w["g_final"], l, seq, l == DEPTH - 1)
        new_conv.append(tail)
        if l == N_A_LAYERS - 1:
            k_rows, v_rows, kv_bf = _kv_proj(x, w["g_kv"], w["w_kv"])
            k_new = k_rows.reshape(batch, seq, SB_HEADS, SB_HEAD_DIM)
            v_new = v_rows.reshape(batch, seq, SB_HEADS, SB_HEAD_DIM)
    y = x.reshape(batch, seq, D_MODEL)
    return (y, jnp.stack(new_c), jnp.stack(new_n), jnp.stack(new_m), jnp.stack(new_conv),
            k_new, v_new)


def kernel(x_prompt, x_sample, cache_k, cache_v, state_mlstm_c, state_mlstm_n, state_mlstm_m,
           state_ffn_conv, g_mix_norm, w_mlstm_in, b_mlstm_gates, g_mlstm_out, w_mlstm_out,
           g_kv_norm, w_kv, w_sb_q, w_sb_o, g_ffn_norm, w_ffn_up, w_ffn_conv, b_ffn_conv,
           w_ffn_down, g_final):
    w = _prep_weights(g_mix_norm, w_mlstm_in, b_mlstm_gates, g_mlstm_out, w_mlstm_out,
                      g_kv_norm, w_kv, w_sb_q, w_sb_o,
                      g_ffn_norm, w_ffn_up, w_ffn_conv, b_ffn_conv, w_ffn_down, g_final)
    bp, tp, d = x_prompt.shape
    bs, ts, _ = x_sample.shape
    past = cache_k.shape[1]

    p_c0 = jnp.zeros((N_A_LAYERS, bp, MLSTM_HEADS, MLSTM_QK_DIM, MLSTM_V_DIM), F32)
    p_n0 = jnp.zeros((N_A_LAYERS, bp, MLSTM_HEADS, MLSTM_QK_DIM), F32)
    p_m0 = jnp.full((N_A_LAYERS, bp, MLSTM_HEADS), NEG_BIG, F32)
    p_conv0 = jnp.zeros((DEPTH, bp, 2, D_FF), F32)
    p_out = _trunk(x_prompt.reshape(bp * tp, d), p_c0, p_n0, p_m0, p_conv0, None, w, bp, tp)

    cache = (cache_k.reshape(bs, past * SB_HEADS, SB_HEAD_DIM),
             cache_v.reshape(bs, past * SB_HEADS, SB_HEAD_DIM))
    s_out = _trunk(x_sample.reshape(bs * ts, d), state_mlstm_c, state_mlstm_n, state_mlstm_m,
                   state_ffn_conv, cache, w, bs, ts)
    return (p_out[0], s_out[0]) + p_out[1:] + s_out[1:]
```

```python
import functools

import jax
import jax.numpy as jnp
from jax import lax
from jax.experimental import pallas as pl
from jax.experimental.pallas import tpu as pltpu

F32 = jnp.float32
BF16 = jnp.bfloat16

D_MODEL = 1024
DEPTH = 4
N_A_LAYERS = 2
MLSTM_HEADS = 8
MLSTM_QK_DIM = 64
MLSTM_V_DIM = 128
MLSTM_PAIRS = MLSTM_HEADS // 2
GATE_SOFTCAP = 15.0
SB_HEADS = 8
SB_HEAD_DIM = 128
D_FF = 2816
NORM_EPS = 1e-6
NEG_BIG = -1e30

LANES = 128
ROW_TILE = 512
FFN_SPLITS = ((0, 1024), (1024, 1024), (2048, 768))
MLSTM_CHUNK = 128
MLSTM_CHUNKS_PER_STEP = 4
SB_BLOCK = 256
SB_DEAD_LOG = -105.0
VMEM_LIMIT = 56 * 1024 * 1024


def _params(n_axes):
    return pltpu.CompilerParams(dimension_semantics=("arbitrary",) * n_axes,
                                vmem_limit_bytes=VMEM_LIMIT)


def _resident(shape):
    zeros = (0,) * len(shape)
    return pl.BlockSpec(shape, lambda *_: zeros, pipeline_mode=pl.Buffered(1))


def _rms(x, g):
    return x * lax.rsqrt(jnp.mean(x * x, axis=-1, keepdims=True) + NORM_EPS) * g


def _dot(a, b):
    return jnp.dot(a, b, preferred_element_type=F32)


def _dot_nt(a, b):
    return lax.dot_general(a, b, (((1,), (1,)), ((), ())), preferred_element_type=F32)


def _split3(x):
    hi = x.astype(BF16)
    r = x - hi.astype(F32)
    mid = r.astype(BF16)
    lo = (r - mid.astype(F32)).astype(BF16)
    return hi, mid, lo


def _log_sigmoid(z):
    return jnp.minimum(z, 0.0) - jnp.log(1.0 + jnp.exp(-jnp.abs(z)))


def _norm_matmul_kernel(x_ref, g_ref, w_ref, *o_refs, col_chunk):
    xn = _rms(x_ref[...], g_ref[...]).astype(BF16)
    n = w_ref.shape[1]
    for c0 in range(0, n, col_chunk):
        y = _dot(xn, w_ref[:, c0:c0 + col_chunk])
        for o_ref in o_refs:
            o_ref[:, c0:c0 + col_chunk] = y.astype(o_ref.dtype)


def _norm_matmul(x, g, w, out_dtypes, name):
    rows, d = x.shape
    n = w.shape[1]
    tm = ROW_TILE
    return pl.pallas_call(
        functools.partial(_norm_matmul_kernel, col_chunk=512),
        grid=(rows // tm,),
        in_specs=[pl.BlockSpec((tm, d), lambda i: (i, 0)), _resident((1, d)), _resident((d, n))],
        out_specs=[pl.BlockSpec((tm, n), lambda i: (i, 0)) for _ in out_dtypes],
        out_shape=[jax.ShapeDtypeStruct((rows, n), dt) for dt in out_dtypes],
        compiler_params=_params(1),
        name=name,
    )(x, g, w)


def _kv_proj_kernel(x_ref, g_ref, w_ref, k_ref, v_ref, kv_ref):
    xn = _rms(x_ref[...], g_ref[...]).astype(BF16)
    tm = x_ref.shape[0]
    group = 4
    for dst, base in ((k_ref, 0), (v_ref, D_MODEL)):
        for h0 in range(0, SB_HEADS, group):
            c0 = base + h0 * SB_HEAD_DIM
            y = _dot(xn, w_ref[:, c0:c0 + group * SB_HEAD_DIM])
            kv_ref[:, c0:c0 + group * SB_HEAD_DIM] = y.astype(kv_ref.dtype)
            for h in range(group):
                dst[pl.ds(h0 + h, tm, stride=SB_HEADS), :] = y[:, h * SB_HEAD_DIM:(h + 1) * SB_HEAD_DIM]


def _kv_proj(x, g, w):
    rows, d = x.shape
    tm = ROW_TILE
    head_rows = pl.BlockSpec((tm * SB_HEADS, SB_HEAD_DIM), lambda i: (i, 0))
    return pl.pallas_call(
        _kv_proj_kernel,
        grid=(rows // tm,),
        in_specs=[pl.BlockSpec((tm, d), lambda i: (i, 0)), _resident((1, d)), _resident((d, 2 * d))],
        out_specs=[head_rows, head_rows, pl.BlockSpec((tm, 2 * d), lambda i: (i, 0))],
        out_shape=[jax.ShapeDtypeStruct((rows * SB_HEADS, SB_HEAD_DIM), F32),
                   jax.ShapeDtypeStruct((rows * SB_HEADS, SB_HEAD_DIM), F32),
                   jax.ShapeDtypeStruct((rows, 2 * d), BF16)],
        compiler_params=_params(1),
        name="kv_proj",
    )(x, g, w)


def _mlstm_in_kernel(x_ref, g_ref, w_ref, wg_ref, bg_ref, o_ref, gate_ref, *, col_chunk):
    xn = _rms(x_ref[...], g_ref[...]).astype(BF16)
    n = w_ref.shape[1]
    for c0 in range(0, n, col_chunk):
        o_ref[:, c0:c0 + col_chunk] = _dot(xn, w_ref[:, c0:c0 + col_chunk]).astype(o_ref.dtype)
    pre = _dot(xn, wg_ref[...]) + bg_ref[...]
    capped = GATE_SOFTCAP * jnp.tanh(pre / GATE_SOFTCAP)
    lane = lax.broadcasted_iota(jnp.int32, capped.shape, 1)
    gate_ref[...] = jnp.where(lane < MLSTM_HEADS, capped, _log_sigmoid(capped))


def _mlstm_in(x, g, w, wg, bg):
    rows, d = x.shape
    n = w.shape[1]
    tm = ROW_TILE
    return pl.pallas_call(
        functools.partial(_mlstm_in_kernel, col_chunk=512),
        grid=(rows // tm,),
        in_specs=[pl.BlockSpec((tm, d), lambda i: (i, 0)), _resident((1, d)), _resident((d, n)),
                  _resident((d, LANES)), _resident((1, LANES))],
        out_specs=[pl.BlockSpec((tm, n), lambda i: (i, 0)),
                   pl.BlockSpec((tm, LANES), lambda i: (i, 0))],
        out_shape=[jax.ShapeDtypeStruct((rows, n), BF16),
                   jax.ShapeDtypeStruct((rows, LANES), F32)],
        compiler_params=_params(1),
        name="mlstm_in_proj",
    )(x, g, w, wg, bg)


def _transpose_rows(a):
    rows = a.shape[0]
    if rows < LANES:
        a = jnp.concatenate([a, jnp.zeros((LANES - rows, LANES), F32)], axis=0)
        return a.T[:, :rows]
    return a.T


def _mlstm_chunk(rows, q_ref, k_ref, v_ref, o_ref, gate_ref, gout_ref, h_ref, c_s, n_s, m_s):
    L = rows.stop - rows.start
    gates = gate_ref[rows, :]
    t_idx = lax.broadcasted_iota(jnp.int32, (L, L), 0)
    s_idx = lax.broadcasted_iota(jnp.int32, (L, L), 1)
    causal = s_idx <= t_idx
    tri = jnp.where(causal, 1.0, 0.0).astype(BF16)
    g_hi, g_mid, g_lo = _split3(gates)
    cum = _dot(tri, g_hi) + _dot(tri, g_mid) + _dot(tri, g_lo)
    gates_t = _transpose_rows(gates)
    cum_t = _transpose_rows(cum)
    lane = lax.broadcasted_iota(jnp.int32, (L, LANES), 1)
    sub = lax.broadcasted_iota(jnp.int32, (LANES, LANES), 0)
    ones = jnp.ones((2 * L, LANES), BF16)
    scale = MLSTM_QK_DIM ** -0.5

    def row_sum(x):
        hi = x.astype(BF16)
        lo = (x - hi.astype(F32)).astype(BF16)
        width = x.shape[1]
        return _dot(hi, ones[:width]) + _dot(lo, ones[:width])

    heads = range(MLSTM_HEADS)
    pairs = range(MLSTM_PAIRS)
    halves = [(lane // MLSTM_QK_DIM) == e for e in range(2)]
    k_pairs = [k_ref[rows, p * LANES:(p + 1) * LANES] for p in pairs]
    c_pairs = [c_s[p] for p in pairs]
    n_pairs = [n_s[p] for p in pairs]
    v_heads = [v_ref[rows, h * MLSTM_V_DIM:(h + 1) * MLSTM_V_DIM] for h in heads]
    m_prevs = [m_s[h:h + 1, :] for h in heads]

    qk, qc, qn = [], [], []
    for p in pairs:
        q_f32 = q_ref[rows, p * LANES:(p + 1) * LANES].astype(F32)
        c_bf = c_pairs[p].astype(BF16)
        n_bf = n_pairs[p].astype(BF16)
        for e in range(2):
            qm = jnp.where(halves[e], q_f32, 0.0).astype(BF16)
            qk.append(_dot_nt(qm, k_pairs[p]))
            qc.append(_dot(qm, c_bf))
            qn.append(_dot(qm, n_bf))

    lis, bs, m_ts, w_inters, ss = [], [], [], [], []
    for h in heads:
        li = jnp.broadcast_to(gates[:, h:h + 1], (L, LANES))
        b = jnp.broadcast_to(cum[:, MLSTM_HEADS + h:MLSTM_HEADS + h + 1], (L, LANES))
        li_row = gates_t[h:h + 1, :]
        b_row = cum_t[MLSTM_HEADS + h:MLSTM_HEADS + h + 1, :]
        log_d = jnp.where(causal, b[:, :L] + (li_row - b_row), NEG_BIG)
        m_inter = b + m_prevs[h]
        m_t = jnp.maximum(m_inter, jnp.max(log_d, axis=-1, keepdims=True))
        lis.append(li)
        bs.append(b)
        m_ts.append(m_t)
        w_inters.append(jnp.exp(m_inter - m_t))
        ss.append(qk[h] * scale * jnp.exp(log_d - m_t[:, :L]))

    sv = [_dot(ss[h].astype(BF16), v_heads[h]) for h in heads]
    s_sum = [row_sum(ss[h]) for h in heads]

    hs = []
    for h in heads:
        num = sv[h] + w_inters[h] * (qc[h] * scale)
        den = s_sum[h] + w_inters[h] * (qn[h] * scale)
        hs.append(num / jnp.maximum(jnp.abs(den), jnp.exp(-m_ts[h])))
    sq_sum = [row_sum(hs[h] * hs[h]) for h in heads]

    kws, decays = [], []
    for h in heads:
        cols = slice(h * MLSTM_V_DIM, (h + 1) * MLSTM_V_DIM)
        hh = hs[h] * lax.rsqrt(sq_sum[h] * (1.0 / MLSTM_V_DIM) + NORM_EPS)
        hh = hh * gout_ref[:, cols] * jax.nn.sigmoid(o_ref[rows, cols].astype(F32))
        h_ref[rows, cols] = hh.astype(h_ref.dtype)
        m_new = m_ts[h][L - 1:L, :]
        b_last = bs[h][L - 1:L, :]
        decays.append(jnp.exp(b_last + m_prevs[h] - m_new))
        w_in = jnp.exp(b_last - bs[h] + lis[h] - m_new)
        kws.append(jnp.where(halves[h % 2], k_pairs[h // 2].astype(F32) * w_in, 0.0))
        m_s[h:h + 1, :] = m_new

    for p in pairs:
        kw = jnp.concatenate([kws[2 * p], kws[2 * p + 1]], axis=0)
        vv = jnp.concatenate([v_heads[2 * p], v_heads[2 * p + 1]], axis=0)
        kw_t = kw.T.astype(BF16)
        decay = jnp.where(sub < MLSTM_QK_DIM, decays[2 * p], decays[2 * p + 1])
        c_s[p] = decay * c_pairs[p] + _dot(kw_t, vv)
        n_s[p] = decay * n_pairs[p] + _dot(kw_t, ones)


def _mlstm_scan_kernel(q_ref, k_ref, v_ref, o_ref, gate_ref, gout_ref, c0_ref, n0_ref, m0_ref,
                       h_ref, c_ref, n_ref, m_ref, c_s, n_s, m_s, *, chunk, chunks):
    j = pl.program_id(1)

    @pl.when(j == 0)
    def _():
        c_s[...] = c0_ref[0]
        n_s[...] = n0_ref[0]
        m_s[...] = m0_ref[0]

    for ci in range(chunks):
        _mlstm_chunk(slice(ci * chunk, (ci + 1) * chunk), q_ref, k_ref, v_ref, o_ref, gate_ref,
                     gout_ref, h_ref, c_s, n_s, m_s)

    @pl.when(j == pl.num_programs(1) - 1)
    def _():
        c_ref[0] = c_s[...]
        n_ref[0] = n_s[...]
        m_ref[0] = m_s[...]


def _mlstm_scan(proj, gates, g_out, c0, n0, m0, batch, seq):
    L = min(MLSTM_CHUNK, seq)
    chunks = min(MLSTM_CHUNKS_PER_STEP, seq // L)
    R = L * chunks
    nchunk = seq // R
    d = D_MODEL
    qk = MLSTM_HEADS * MLSTM_QK_DIM

    def row(b, j):
        return b * nchunk + j

    return pl.pallas_call(
        functools.partial(_mlstm_scan_kernel, chunk=L, chunks=chunks),
        grid=(batch, nchunk),
        in_specs=[
            pl.BlockSpec((R, qk), lambda b, j: (row(b, j), 0)),
            pl.BlockSpec((R, qk), lambda b, j: (row(b, j), 1)),
            pl.BlockSpec((R, d), lambda b, j: (row(b, j), 1)),
            pl.BlockSpec((R, d), lambda b, j: (row(b, j), 2)),
            pl.BlockSpec((R, LANES), lambda b, j: (row(b, j), 0)),
            _resident((1, d)),
            pl.BlockSpec((1, MLSTM_PAIRS, LANES, LANES), lambda b, j: (b, 0, 0, 0)),
            pl.BlockSpec((1, MLSTM_PAIRS, LANES, LANES), lambda b, j: (b, 0, 0, 0)),
            pl.BlockSpec((1, MLSTM_HEADS, LANES), lambda b, j: (b, 0, 0)),
        ],
        out_specs=[
            pl.BlockSpec((R, d), lambda b, j: (row(b, j), 0)),
            pl.BlockSpec((1, MLSTM_PAIRS, LANES, LANES), lambda b, j: (b, 0, 0, 0)),
            pl.BlockSpec((1, MLSTM_PAIRS, LANES, LANES), lambda b, j: (b, 0, 0, 0)),
            pl.BlockSpec((1, MLSTM_HEADS, LANES), lambda b, j: (b, 0, 0)),
        ],
        out_shape=[
            jax.ShapeDtypeStruct((batch * seq, d), BF16),
            jax.ShapeDtypeStruct((batch, MLSTM_PAIRS, LANES, LANES), F32),
            jax.ShapeDtypeStruct((batch, MLSTM_PAIRS, LANES, LANES), F32),
            jax.ShapeDtypeStruct((batch, MLSTM_HEADS, LANES), F32),
        ],
        scratch_shapes=[pltpu.VMEM((MLSTM_PAIRS, LANES, LANES), F32),
                        pltpu.VMEM((MLSTM_PAIRS, LANES, LANES), F32),
                        pltpu.VMEM((MLSTM_HEADS, LANES), F32)],
        compiler_params=_params(2),
        name="mlstm_scan",
    )(proj, proj, proj, proj, gates, g_out, c0, n0, m0)


def _ffn_kernel(x_ref, y_ref, wo_ref, g_ref, wu_ref, wc_ref, bc_ref, wd_ref, prev_ref, gf_ref,
                o_ref, tail_ref, xn_s, acc_s, carry_s, a_s, v_s, *, seq, tm, final_norm):
    sub_len = min(seq, tm)
    n_sub = tm // sub_len
    tiles_per_seq = max(1, seq // tm)
    i = pl.program_id(0)
    starts_seq = (i % tiles_per_seq) == 0

    x = x_ref[...] + _dot(y_ref[...], wo_ref[...])
    xn_s[...] = _rms(x, g_ref[...]).astype(BF16)
    acc_s[...] = x
    row = lax.broadcasted_iota(jnp.int32, (sub_len, 1), 0)

    if n_sub == 1:
        @pl.when(starts_seq)
        def _():
            carry_s[...] = prev_ref[0]

    def up(k):
        c0, n = FFN_SPLITS[k]
        xn = xn_s[...]
        a_s[:, c0:c0 + n] = _dot(xn, wu_ref[:, c0:c0 + n])
        v_s[:, c0:c0 + n] = _dot(xn, wu_ref[:, D_FF + c0:D_FF + c0 + n])

    def down(k):
        c0, n = FFN_SPLITS[k]
        cols = slice(c0, c0 + n)
        w = wc_ref[:, cols]
        b = bc_ref[:, cols]
        hs = []
        for s in range(n_sub):
            rows = slice(s * sub_len, (s + 1) * sub_len)
            a = a_s[rows, cols]
            prev = carry_s[:, cols] if n_sub == 1 else prev_ref[s, :, cols]
            p0, p1 = prev[0:1], prev[1:2]
            am1 = jnp.where(row == 0, p1, pltpu.roll(a, 1, 0))
            am2 = jnp.where(row == 0, p0, jnp.where(row == 1, p1, pltpu.roll(a, 2, 0)))
            conv = ((b + w[0:1] * am2) + w[1:2] * am1) + w[2:3] * a
            hs.append((jax.nn.gelu(conv) * v_s[rows, cols]).astype(BF16))
            tail = a[sub_len - 2:sub_len]
            tail_ref[s, :, cols] = tail
            if n_sub == 1:
                carry_s[:, cols] = tail
        h = hs[0] if n_sub == 1 else jnp.concatenate(hs, axis=0)
        acc_s[...] += _dot(h, wd_ref[cols, :])

    up(0)
    for k in range(len(FFN_SPLITS)):
        if k + 1 < len(FFN_SPLITS):
            up(k + 1)
        down(k)
    out = acc_s[...]
    o_ref[...] = _rms(out, gf_ref[...]) if final_norm else out


def _ffn(x, y, wo, g, w_up, wc, bc, w_dn, prev, g_final, layer, seq, final_norm):
    rows, d = x.shape
    tm = ROW_TILE
    batch = rows // seq
    if seq >= tm:
        seqs_per_tile = 1
        tiles_per_seq = seq // tm
        seq_block = lambda i: (i // tiles_per_seq, 0, 0)
    else:
        seqs_per_tile = tm // seq
        seq_block = lambda i: (i, 0, 0)
    state_spec = pl.BlockSpec((seqs_per_tile, 2, D_FF), seq_block)
    up_spec = pl.BlockSpec((None, d, 2 * D_FF), lambda i: (layer, 0, 0), pipeline_mode=pl.Buffered(1))
    down_spec = pl.BlockSpec((None, D_FF, d), lambda i: (layer, 0, 0), pipeline_mode=pl.Buffered(1))
    return pl.pallas_call(
        functools.partial(_ffn_kernel, seq=seq, tm=tm, final_norm=final_norm),
        grid=(rows // tm,),
        in_specs=[pl.BlockSpec((tm, d), lambda i: (i, 0)), pl.BlockSpec((tm, d), lambda i: (i, 0)),
                  _resident((d, d)), _resident((1, d)), up_spec,
                  _resident((3, D_FF)), _resident((1, D_FF)), down_spec, state_spec,
                  _resident((1, d))],
        out_specs=[pl.BlockSpec((tm, d), lambda i: (i, 0)), state_spec],
        out_shape=[jax.ShapeDtypeStruct((rows, d), F32),
                   jax.ShapeDtypeStruct((batch, 2, D_FF), F32)],
        scratch_shapes=[pltpu.VMEM((tm, d), BF16), pltpu.VMEM((tm, d), F32),
                        pltpu.VMEM((2, D_FF), F32),
                        pltpu.VMEM((tm, D_FF), F32), pltpu.VMEM((tm, D_FF), F32)],
        compiler_params=_params(1),
        name="conv_ffn",
    )(x, y, wo, g, w_up, wc, bc, w_dn, prev, g_final)


def _later_matrix(bk):
    j_idx = lax.broadcasted_iota(jnp.int32, (bk, bk), 0)
    s_idx = lax.broadcasted_iota(jnp.int32, (bk, bk), 1)
    return jnp.where(j_idx > s_idx, 1.0, 0.0).astype(BF16)


def _sb_visit(qs, kbs, vbs, state, later, strict_lower):
    heads = range(len(qs))
    zs = [_dot_nt(qs[h], kbs[h]) for h in heads]
    log_betas, log_keeps = [], []
    for h in heads:
        z = (zs[h] * (SB_HEAD_DIM ** -0.5)).astype(BF16)
        log_beta = _log_sigmoid(z)
        log_keep = log_beta - z
        if strict_lower is not None:
            log_keep = jnp.where(strict_lower, log_keep, jnp.zeros_like(log_keep))
        log_betas.append(log_beta)
        log_keeps.append(log_keep)
    afters = [_dot(log_keeps[h], later) for h in heads]
    weights, runs = [], []
    for h in heads:
        a = jnp.exp(log_betas[h].astype(F32) + afters[h] + state[h][0])
        if strict_lower is not None:
            a = jnp.where(strict_lower, a, 0.0)
        weights.append(a.astype(BF16))
        runs.append(state[h][0] + afters[h][:, 0:1] + log_keeps[h][:, 0:1].astype(F32))
    return tuple((runs[h], state[h][1] + _dot(weights[h], vbs[h])) for h in heads)


def _sb_alive(state):
    top = state[0][0]
    for run, _ in state[1:]:
        top = jnp.maximum(top, run)
    return jnp.max(top) > SB_DEAD_LOG


def _sb_prompt_kernel(x_ref, g_ref, wq_ref, k_ref, v_ref, o_ref, *, blk, heads):
    i = pl.program_id(1)
    dh = SB_HEAD_DIM
    cols = [slice(h * dh, (h + 1) * dh) for h in range(heads)]
    xn = _rms(x_ref[...], g_ref[...]).astype(BF16)
    q_all = _dot(xn, wq_ref[...]).astype(BF16)
    qs = [q_all[:, c] for c in cols]
    t_idx = lax.broadcasted_iota(jnp.int32, (blk, blk), 0)
    s_idx = lax.broadcasted_iota(jnp.int32, (blk, blk), 1)
    later = _later_matrix(blk)

    def visit(off, state, mask):
        return _sb_visit(qs, [k_ref[pl.ds(off, blk), c] for c in cols],
                         [v_ref[pl.ds(off, blk), c] for c in cols], state, later, mask)

    state = tuple((jnp.zeros((blk, 1), F32), jnp.zeros((blk, dh), F32)) for _ in range(heads))
    state = visit(pl.multiple_of(i * blk, blk), state, s_idx < t_idx)

    def cond(carry):
        n, alive, _ = carry
        return jnp.logical_and(n < i, alive)

    def body(carry):
        n, _, state = carry
        state = visit(pl.multiple_of((i - 1 - n) * blk, blk), state, None)
        return n + 1, _sb_alive(state), state

    _, _, state = lax.while_loop(cond, body, (jnp.int32(0), _sb_alive(state), state))
    for h in range(heads):
        o_ref[:, cols[h]] = state[h][1].astype(o_ref.dtype)


def _sb_prompt(x, g, wq, kv, batch, seq):
    blk = SB_BLOCK
    nq = seq // blk
    d = D_MODEL
    return pl.pallas_call(
        functools.partial(_sb_prompt_kernel, blk=blk, heads=SB_HEADS),
        grid=(batch, nq),
        in_specs=[pl.BlockSpec((blk, d), lambda b, i: (b * nq + i, 0)),
                  _resident((1, d)), _resident((d, d)),
                  pl.BlockSpec((seq, d), lambda b, i: (b, 0)),
                  pl.BlockSpec((seq, d), lambda b, i: (b, 1))],
        out_specs=pl.BlockSpec((blk, d), lambda b, i: (b * nq + i, 0)),
        out_shape=jax.ShapeDtypeStruct((batch * seq, d), BF16),
        compiler_params=_params(2),
        name="sb_attention_prompt",
    )(x, g, wq, kv, kv)


def _sb_cached_kernel(q_ref, kv_ref, kc_hbm, vc_hbm, o_ref, kbuf, vbuf, sems, *, seq, blk, n_blocks):
    b = pl.program_id(0)
    dh = SB_HEAD_DIM
    heads = SB_HEADS
    rows = blk * heads
    cols = [slice(h * dh, (h + 1) * dh) for h in range(heads)]
    qs = [q_ref[:, c] for c in cols]

    def copies(stream, n, slot):
        off = pl.multiple_of((n_blocks - 1 - n) * rows, rows)
        return (pltpu.make_async_copy(kc_hbm.at[stream, pl.ds(off, rows), :], kbuf.at[slot],
                                      sems.at[0, slot]),
                pltpu.make_async_copy(vc_hbm.at[stream, pl.ds(off, rows), :], vbuf.at[slot],
                                      sems.at[1, slot]))

    def start(stream, n, slot):
        for c in copies(stream, n, slot):
            c.start()

    def wait(n, slot):
        for c in copies(b, n, slot):
            c.wait()

    @pl.when(b == 0)
    def _():
        start(b, 0, 0)

    t_idx = lax.broadcasted_iota(jnp.int32, (seq, seq), 0)
    s_idx = lax.broadcasted_iota(jnp.int32, (seq, seq), 1)
    state = tuple((jnp.zeros((seq, 1), F32), jnp.zeros((seq, dh), F32)) for _ in range(heads))
    state = _sb_visit(qs, [kv_ref[:, c] for c in cols],
                      [kv_ref[:, D_MODEL + h * dh:D_MODEL + (h + 1) * dh] for h in range(heads)],
                      state, _later_matrix(seq), s_idx < t_idx)
    later = _later_matrix(blk)

    def cond(carry):
        n, alive, _ = carry
        return jnp.logical_and(n < n_blocks, alive)

    def body(carry):
        n, _, state = carry
        slot = n % 2
        wait(n, slot)

        @pl.when(jnp.logical_and(n >= 1, n + 1 < n_blocks))
        def _():
            start(b, n + 1, 1 - slot)

        state = _sb_visit(
            qs, [kbuf[slot, pl.ds(h, blk, stride=heads), :].astype(BF16) for h in range(heads)],
            [vbuf[slot, pl.ds(h, blk, stride=heads), :].astype(BF16) for h in range(heads)],
            state, later, None)
        alive = _sb_alive(state)

        if n_blocks > 1:
            @pl.when(jnp.logical_and(n == 0, alive))
            def _():
                start(b, 1, 1)

        return n + 1, alive, state

    n_done, _, state = lax.while_loop(cond, body, (jnp.int32(0), _sb_alive(state), state))

    @pl.when(jnp.logical_or(n_done == 0, jnp.logical_and(n_done >= 2, n_done < n_blocks)))
    def _():
        wait(n_done, n_done % 2)

    @pl.when(b + 1 < pl.num_programs(0))
    def _():
        start(b + 1, 0, 0)

    for h in range(heads):
        o_ref[:, cols[h]] = state[h][1].astype(o_ref.dtype)


def _sb_cached(q, kv, cache_k, cache_v, batch, seq):
    dh = SB_HEAD_DIM
    blk = SB_BLOCK
    rows = blk * SB_HEADS
    n_blocks = cache_k.shape[1] // rows
    return pl.pallas_call(
        functools.partial(_sb_cached_kernel, seq=seq, blk=blk, n_blocks=n_blocks),
        grid=(batch,),
        in_specs=[pl.BlockSpec((seq, D_MODEL), lambda b: (b, 0)),
                  pl.BlockSpec((seq, 2 * D_MODEL), lambda b: (b, 0)),
                  pl.BlockSpec(memory_space=pl.ANY),
                  pl.BlockSpec(memory_space=pl.ANY)],
        out_specs=pl.BlockSpec((seq, D_MODEL), lambda b: (b, 0)),
        out_shape=jax.ShapeDtypeStruct((batch * seq, D_MODEL), BF16),
        scratch_shapes=[pltpu.VMEM((2, rows, dh), F32), pltpu.VMEM((2, rows, dh), F32),
                        pltpu.SemaphoreType.DMA((2, 2))],
        compiler_params=_params(1),
        name="sb_attention_cached",
    )(q, kv, cache_k, cache_v)


def _prep_weights(g_mix_norm, w_mlstm_in, b_mlstm_gates, g_mlstm_out, w_mlstm_out,
                  g_kv_norm, w_kv, w_sb_q, w_sb_o,
                  g_ffn_norm, w_ffn_up, w_ffn_conv, b_ffn_conv, w_ffn_down, g_final):
    main = 2 * MLSTM_HEADS * MLSTM_QK_DIM + MLSTM_HEADS * MLSTM_V_DIM + D_MODEL
    n_gate = 2 * MLSTM_HEADS
    pad = LANES - n_gate
    return dict(
        g_mix=g_mix_norm.reshape(DEPTH, 1, D_MODEL),
        w_in=w_mlstm_in[:, :, :main].astype(BF16),
        w_gate=jnp.pad(w_mlstm_in[:, :, main:], ((0, 0), (0, 0), (0, pad))).astype(BF16),
        b_gate=jnp.pad(b_mlstm_gates, ((0, 0), (0, pad))).reshape(N_A_LAYERS, 1, LANES),
        g_out=g_mlstm_out.reshape(N_A_LAYERS, 1, D_MODEL),
        w_out=w_mlstm_out.astype(BF16),
        g_kv=g_kv_norm.reshape(1, D_MODEL),
        w_kv=w_kv.astype(BF16),
        w_q=w_sb_q.astype(BF16),
        w_o=w_sb_o.astype(BF16),
        g_ffn=g_ffn_norm.reshape(DEPTH, 1, D_MODEL),
        w_up=w_ffn_up.astype(BF16),
        w_c=w_ffn_conv,
        b_c=b_ffn_conv.reshape(DEPTH, 1, D_FF),
        w_d=w_ffn_down.astype(BF16),
        g_final=g_final.reshape(1, D_MODEL),
    )


def _trunk(x, c0, n0, m0, conv0, cache, w, batch, seq):
    new_c, new_n, new_m, new_conv = [], [], [], []
    kv_bf = k_new = v_new = None
    for l in range(DEPTH):
        if l < N_A_LAYERS:
            proj, gates = _mlstm_in(x, w["g_mix"][l], w["w_in"][l], w["w_gate"][l], w["b_gate"][l])
            mixed, c, n, m = _mlstm_scan(
                proj, gates, w["g_out"][l],
                c0[l].reshape(batch, MLSTM_PAIRS, LANES, LANES),
                jnp.broadcast_to(n0[l].reshape(batch, MLSTM_PAIRS, LANES, 1),
                                 (batch, MLSTM_PAIRS, LANES, LANES)),
                jnp.broadcast_to(m0[l].reshape(batch, MLSTM_HEADS, 1), (batch, MLSTM_HEADS, LANES)),
                batch, seq)
            w_mix = w["w_out"][l]
            new_c.append(c.reshape(batch, MLSTM_HEADS, MLSTM_QK_DIM, MLSTM_V_DIM))
            new_n.append(n[..., 0].reshape(batch, MLSTM_HEADS, MLSTM_QK_DIM))
            new_m.append(m[..., 0])
        else:
            j = l - N_A_LAYERS
            if cache is None:
                mixed = _sb_prompt(x, w["g_mix"][l], w["w_q"][j], kv_bf, batch, seq)
            else:
                (q,) = _norm_matmul(x, w["g_mix"][l], w["w_q"][j], [BF16], "sb_q_proj")
                mixed = _sb_cached(q, kv_bf, cache[0], cache[1], batch, seq)
            w_mix = w["w_o"][j]
        x, tail = _ffn(x, mixed, w_mix, w["g_ffn"][l], w["w_up"], w["w_c"][l], w["b_c"][l],
                       w["w_d"], conv0[l], w["g_final"], l, seq, l == DEPTH - 1)
        new_conv.append(tail)
        if l == N_A_LAYERS - 1:
            k_rows, v_rows, kv_bf = _kv_proj(x, w["g_kv"], w["w_kv"])
            k_new = k_rows.reshape(batch, seq, SB_HEADS, SB_HEAD_DIM)
            v_new = v_rows.reshape(batch, seq, SB_HEADS, SB_HEAD_DIM)
    y = x.reshape(batch, seq, D_MODEL)
    return (y, jnp.stack(new_c), jnp.stack(new_n), jnp.stack(new_m), jnp.stack(new_conv),
            k_new, v_new)


def kernel(x_prompt, x_sample, cache_k, cache_v, state_mlstm_c, state_mlstm_n, state_mlstm_m,
           state_ffn_conv, g_mix_norm, w_mlstm_in, b_mlstm_gates, g_mlstm_out, w_mlstm_out,
           g_kv_norm, w_kv, w_sb_q, w_sb_o, g_ffn_norm, w_ffn_up, w_ffn_conv, b_ffn_conv,
           w_ffn_down, g_final):
    w = _prep_weights(g_mix_norm, w_mlstm_in, b_mlstm_gates, g_mlstm_out, w_mlstm_out,
                      g_kv_norm, w_kv, w_sb_q, w_sb_o,
                      g_ffn_norm, w_ffn_up, w_ffn_conv, b_ffn_conv, w_ffn_down, g_final)
    bp, tp, d = x_prompt.shape
    bs, ts, _ = x_sample.shape
    past = cache_k.shape[1]

    p_c0 = jnp.zeros((N_A_LAYERS, bp, MLSTM_HEADS, MLSTM_QK_DIM, MLSTM_V_DIM), F32)
    p_n0 = jnp.zeros((N_A_LAYERS, bp, MLSTM_HEADS, MLSTM_QK_DIM), F32)
    p_m0 = jnp.full((N_A_LAYERS, bp, MLSTM_HEADS), NEG_BIG, F32)
    p_conv0 = jnp.zeros((DEPTH, bp, 2, D_FF), F32)
    p_out = _trunk(x_prompt.reshape(bp * tp, d), p_c0, p_n0, p_m0, p_conv0, None, w, bp, tp)

    cache = (cache_k.reshape(bs, past * SB_HEADS, SB_HEAD_DIM),
             cache_v.reshape(bs, past * SB_HEADS, SB_HEAD_DIM))
    s_out = _trunk(x_sample.reshape(bs * ts, d), state_mlstm_c, state_mlstm_n, state_mlstm_m,
                   state_ffn_conv, cache, w, bs, ts)
    return (p_out[0], s_out[0]) + p_out[1:] + s_out[1:]
```

```python
import functools

import jax
import jax.numpy as jnp
from jax import lax
from jax.experimental import pallas as pl
from jax.experimental.pallas import tpu as pltpu

F32 = jnp.float32
BF16 = jnp.bfloat16

D_MODEL = 1024
DEPTH = 4
N_A_LAYERS = 2
MLSTM_HEADS = 8
MLSTM_QK_DIM = 64
MLSTM_V_DIM = 128
MLSTM_PAIRS = MLSTM_HEADS // 2
GATE_SOFTCAP = 15.0
SB_HEADS = 8
SB_HEAD_DIM = 128
D_FF = 2816
NORM_EPS = 1e-6
NEG_BIG = -1e30

LANES = 128
ROW_TILE = 512
FFN_SPLITS = ((0, 1024), (1024, 1024), (2048, 768))
MLSTM_CHUNK = 128
MLSTM_CHUNKS_PER_STEP = 8
SB_BLOCK = 256
SB_DEAD_LOG = -105.0
VMEM_LIMIT = 56 * 1024 * 1024


def _params(n_axes):
    return pltpu.CompilerParams(dimension_semantics=("arbitrary",) * n_axes,
                                vmem_limit_bytes=VMEM_LIMIT)


def _resident(shape):
    zeros = (0,) * len(shape)
    return pl.BlockSpec(shape, lambda *_: zeros, pipeline_mode=pl.Buffered(1))


def _rms(x, g):
    return x * lax.rsqrt(jnp.mean(x * x, axis=-1, keepdims=True) + NORM_EPS) * g


def _dot(a, b):
    return jnp.dot(a, b, preferred_element_type=F32)


def _dot_nt(a, b):
    return lax.dot_general(a, b, (((1,), (1,)), ((), ())), preferred_element_type=F32)


def _split3(x):
    hi = x.astype(BF16)
    r = x - hi.astype(F32)
    mid = r.astype(BF16)
    lo = (r - mid.astype(F32)).astype(BF16)
    return hi, mid, lo


def _log_sigmoid(z):
    return jnp.minimum(z, 0.0) - jnp.log(1.0 + jnp.exp(-jnp.abs(z)))


def _norm_matmul_kernel(x_ref, g_ref, w_ref, *o_refs, col_chunk):
    xn = _rms(x_ref[...], g_ref[...]).astype(BF16)
    n = w_ref.shape[1]
    for c0 in range(0, n, col_chunk):
        y = _dot(xn, w_ref[:, c0:c0 + col_chunk])
        for o_ref in o_refs:
            o_ref[:, c0:c0 + col_chunk] = y.astype(o_ref.dtype)


def _norm_matmul(x, g, w, out_dtypes, name):
    rows, d = x.shape
    n = w.shape[1]
    tm = ROW_TILE
    return pl.pallas_call(
        functools.partial(_norm_matmul_kernel, col_chunk=512),
        grid=(rows // tm,),
        in_specs=[pl.BlockSpec((tm, d), lambda i: (i, 0)), _resident((1, d)), _resident((d, n))],
        out_specs=[pl.BlockSpec((tm, n), lambda i: (i, 0)) for _ in out_dtypes],
        out_shape=[jax.ShapeDtypeStruct((rows, n), dt) for dt in out_dtypes],
        compiler_params=_params(1),
        name=name,
    )(x, g, w)


def _kv_proj_kernel(x_ref, g_ref, w_ref, k_ref, v_ref, kv_ref):
    xn = _rms(x_ref[...], g_ref[...]).astype(BF16)
    tm = x_ref.shape[0]
    group = 4
    for dst, base in ((k_ref, 0), (v_ref, D_MODEL)):
        for h0 in range(0, SB_HEADS, group):
            c0 = base + h0 * SB_HEAD_DIM
            y = _dot(xn, w_ref[:, c0:c0 + group * SB_HEAD_DIM])
            kv_ref[:, c0:c0 + group * SB_HEAD_DIM] = y.astype(kv_ref.dtype)
            for h in range(group):
                dst[pl.ds(h0 + h, tm, stride=SB_HEADS), :] = y[:, h * SB_HEAD_DIM:(h + 1) * SB_HEAD_DIM]


def _kv_proj(x, g, w):
    rows, d = x.shape
    tm = ROW_TILE
    head_rows = pl.BlockSpec((tm * SB_HEADS, SB_HEAD_DIM), lambda i: (i, 0))
    return pl.pallas_call(
        _kv_proj_kernel,
        grid=(rows // tm,),
        in_specs=[pl.BlockSpec((tm, d), lambda i: (i, 0)), _resident((1, d)), _resident((d, 2 * d))],
        out_specs=[head_rows, head_rows, pl.BlockSpec((tm, 2 * d), lambda i: (i, 0))],
        out_shape=[jax.ShapeDtypeStruct((rows * SB_HEADS, SB_HEAD_DIM), F32),
                   jax.ShapeDtypeStruct((rows * SB_HEADS, SB_HEAD_DIM), F32),
                   jax.ShapeDtypeStruct((rows, 2 * d), BF16)],
        compiler_params=_params(1),
        name="kv_proj",
    )(x, g, w)


def _mlstm_in_kernel(x_ref, g_ref, w_ref, wg_ref, bg_ref, o_ref, gate_ref, *, col_chunk):
    xn = _rms(x_ref[...], g_ref[...]).astype(BF16)
    n = w_ref.shape[1]
    for c0 in range(0, n, col_chunk):
        o_ref[:, c0:c0 + col_chunk] = _dot(xn, w_ref[:, c0:c0 + col_chunk]).astype(o_ref.dtype)
    pre = _dot(xn, wg_ref[...]) + bg_ref[...]
    capped = GATE_SOFTCAP * jnp.tanh(pre / GATE_SOFTCAP)
    lane = lax.broadcasted_iota(jnp.int32, capped.shape, 1)
    gate_ref[...] = jnp.where(lane < MLSTM_HEADS, capped, _log_sigmoid(capped))


def _mlstm_in(x, g, w, wg, bg):
    rows, d = x.shape
    n = w.shape[1]
    tm = ROW_TILE
    return pl.pallas_call(
        functools.partial(_mlstm_in_kernel, col_chunk=512),
        grid=(rows // tm,),
        in_specs=[pl.BlockSpec((tm, d), lambda i: (i, 0)), _resident((1, d)), _resident((d, n)),
                  _resident((d, LANES)), _resident((1, LANES))],
        out_specs=[pl.BlockSpec((tm, n), lambda i: (i, 0)),
                   pl.BlockSpec((tm, LANES), lambda i: (i, 0))],
        out_shape=[jax.ShapeDtypeStruct((rows, n), BF16),
                   jax.ShapeDtypeStruct((rows, LANES), F32)],
        compiler_params=_params(1),
        name="mlstm_in_proj",
    )(x, g, w, wg, bg)


def _transpose_rows(a):
    rows = a.shape[0]
    if rows < LANES:
        a = jnp.concatenate([a, jnp.zeros((LANES - rows, LANES), F32)], axis=0)
        return a.T[:, :rows]
    return a.T


def _mlstm_chunk(rows, q_ref, k_ref, v_ref, o_ref, gate_ref, gout_ref, h_ref, c_s, n_s, m_s):
    L = rows.stop - rows.start
    gates = gate_ref[rows, :]
    t_idx = lax.broadcasted_iota(jnp.int32, (L, L), 0)
    s_idx = lax.broadcasted_iota(jnp.int32, (L, L), 1)
    causal = s_idx <= t_idx
    tri = jnp.where(causal, 1.0, 0.0).astype(BF16)
    g_hi, g_mid, g_lo = _split3(gates)
    cum = _dot(tri, g_hi) + _dot(tri, g_mid) + _dot(tri, g_lo)
    gates_t = _transpose_rows(gates)
    cum_t = _transpose_rows(cum)
    lane = lax.broadcasted_iota(jnp.int32, (L, LANES), 1)
    sub = lax.broadcasted_iota(jnp.int32, (LANES, LANES), 0)
    ones = jnp.ones((2 * L, LANES), BF16)
    scale = MLSTM_QK_DIM ** -0.5

    def row_sum(x):
        hi = x.astype(BF16)
        lo = (x - hi.astype(F32)).astype(BF16)
        width = x.shape[1]
        return _dot(hi, ones[:width]) + _dot(lo, ones[:width])

    heads = range(MLSTM_HEADS)
    pairs = range(MLSTM_PAIRS)
    halves = [(lane // MLSTM_QK_DIM) == e for e in range(2)]
    k_pairs = [k_ref[rows, p * LANES:(p + 1) * LANES] for p in pairs]
    c_pairs = [c_s[p] for p in pairs]
    n_pairs = [n_s[p] for p in pairs]
    v_heads = [v_ref[rows, h * MLSTM_V_DIM:(h + 1) * MLSTM_V_DIM] for h in heads]
    m_prevs = [m_s[h:h + 1, :] for h in heads]

    qk, qc, qn = [], [], []
    for p in pairs:
        q_f32 = q_ref[rows, p * LANES:(p + 1) * LANES].astype(F32)
        c_bf = c_pairs[p].astype(BF16)
        n_bf = n_pairs[p].astype(BF16)
        for e in range(2):
            qm = jnp.where(halves[e], q_f32, 0.0).astype(BF16)
            qk.append(_dot_nt(qm, k_pairs[p]))
            qc.append(_dot(qm, c_bf))
            qn.append(_dot(qm, n_bf))

    lis, bs, m_ts, w_inters, ss = [], [], [], [], []
    for h in heads:
        li = jnp.broadcast_to(gates[:, h:h + 1], (L, LANES))
        b = jnp.broadcast_to(cum[:, MLSTM_HEADS + h:MLSTM_HEADS + h + 1], (L, LANES))
        li_row = gates_t[h:h + 1, :]
        b_row = cum_t[MLSTM_HEADS + h:MLSTM_HEADS + h + 1, :]
        log_d = jnp.where(causal, b[:, :L] + (li_row - b_row), NEG_BIG)
        m_inter = b + m_prevs[h]
        m_t = jnp.maximum(m_inter, jnp.max(log_d, axis=-1, keepdims=True))
        lis.append(li)
        bs.append(b)
        m_ts.append(m_t)
        w_inters.append(jnp.exp(m_inter - m_t))
        ss.append(qk[h] * scale * jnp.exp(log_d - m_t[:, :L]))

    sv = [_dot(ss[h].astype(BF16), v_heads[h]) for h in heads]
    s_sum = [row_sum(ss[h]) for h in heads]

    hs = []
    for h in heads:
        num = sv[h] + w_inters[h] * (qc[h] * scale)
        den = s_sum[h] + w_inters[h] * (qn[h] * scale)
        hs.append(num / jnp.maximum(jnp.abs(den), jnp.exp(-m_ts[h])))
    sq_sum = [row_sum(hs[h] * hs[h]) for h in heads]

    kws, decays = [], []
    for h in heads:
        cols = slice(h * MLSTM_V_DIM, (h + 1) * MLSTM_V_DIM)
        hh = hs[h] * lax.rsqrt(sq_sum[h] * (1.0 / MLSTM_V_DIM) + NORM_EPS)
        hh = hh * gout_ref[:, cols] * jax.nn.sigmoid(o_ref[rows, cols].astype(F32))
        h_ref[rows, cols] = hh.astype(h_ref.dtype)
        m_new = m_ts[h][L - 1:L, :]
        b_last = bs[h][L - 1:L, :]
        decays.append(jnp.exp(b_last + m_prevs[h] - m_new))
        w_in = jnp.exp(b_last - bs[h] + lis[h] - m_new)
        kws.append(jnp.where(halves[h % 2], k_pairs[h // 2].astype(F32) * w_in, 0.0))
        m_s[h:h + 1, :] = m_new

    for p in pairs:
        kw = jnp.concatenate([kws[2 * p], kws[2 * p + 1]], axis=0)
        vv = jnp.concatenate([v_heads[2 * p], v_heads[2 * p + 1]], axis=0)
        kw_t = kw.T.astype(BF16)
        decay = jnp.where(sub < MLSTM_QK_DIM, decays[2 * p], decays[2 * p + 1])
        c_s[p] = decay * c_pairs[p] + _dot(kw_t, vv)
        n_s[p] = decay * n_pairs[p] + _dot(kw_t, ones)


def _mlstm_scan_kernel(q_ref, k_ref, v_ref, o_ref, gate_ref, gout_ref, c0_ref, n0_ref, m0_ref,
                       h_ref, c_ref, n_ref, m_ref, c_s, n_s, m_s, *, chunk, chunks):
    j = pl.program_id(1)

    @pl.when(j == 0)
    def _():
        c_s[...] = c0_ref[0]
        n_s[...] = n0_ref[0]
        m_s[...] = m0_ref[0]

    for ci in range(chunks):
        _mlstm_chunk(slice(ci * chunk, (ci + 1) * chunk), q_ref, k_ref, v_ref, o_ref, gate_ref,
                     gout_ref, h_ref, c_s, n_s, m_s)

    @pl.when(j == pl.num_programs(1) - 1)
    def _():
        c_ref[0] = c_s[...]
        n_ref[0] = n_s[...]
        m_ref[0] = m_s[...]


def _mlstm_scan(proj, gates, g_out, c0, n0, m0, batch, seq):
    L = min(MLSTM_CHUNK, seq)
    chunks = min(MLSTM_CHUNKS_PER_STEP, seq // L)
    R = L * chunks
    nchunk = seq // R
    d = D_MODEL
    qk = MLSTM_HEADS * MLSTM_QK_DIM

    def row(b, j):
        return b * nchunk + j

    return pl.pallas_call(
        functools.partial(_mlstm_scan_kernel, chunk=L, chunks=chunks),
        grid=(batch, nchunk),
        in_specs=[
            pl.BlockSpec((R, qk), lambda b, j: (row(b, j), 0)),
            pl.BlockSpec((R, qk), lambda b, j: (row(b, j), 1)),
            pl.BlockSpec((R, d), lambda b, j: (row(b, j), 1)),
            pl.BlockSpec((R, d), lambda b, j: (row(b, j), 2)),
            pl.BlockSpec((R, LANES), lambda b, j: (row(b, j), 0)),
            _resident((1, d)),
            pl.BlockSpec((1, MLSTM_PAIRS, LANES, LANES), lambda b, j: (b, 0, 0, 0)),
            pl.BlockSpec((1, MLSTM_PAIRS, LANES, LANES), lambda b, j: (b, 0, 0, 0)),
            pl.BlockSpec((1, MLSTM_HEADS, LANES), lambda b, j: (b, 0, 0)),
        ],
        out_specs=[
            pl.BlockSpec((R, d), lambda b, j: (row(b, j), 0)),
            pl.BlockSpec((1, MLSTM_PAIRS, LANES, LANES), lambda b, j: (b, 0, 0, 0)),
            pl.BlockSpec((1, MLSTM_PAIRS, LANES, LANES), lambda b, j: (b, 0, 0, 0)),
            pl.BlockSpec((1, MLSTM_HEADS, LANES), lambda b, j: (b, 0, 0)),
        ],
        out_shape=[
            jax.ShapeDtypeStruct((batch * seq, d), BF16),
            jax.ShapeDtypeStruct((batch, MLSTM_PAIRS, LANES, LANES), F32),
            jax.ShapeDtypeStruct((batch, MLSTM_PAIRS, LANES, LANES), F32),
            jax.ShapeDtypeStruct((batch, MLSTM_HEADS, LANES), F32),
        ],
        scratch_shapes=[pltpu.VMEM((MLSTM_PAIRS, LANES, LANES), F32),
                        pltpu.VMEM((MLSTM_PAIRS, LANES, LANES), F32),
                        pltpu.VMEM((MLSTM_HEADS, LANES), F32)],
        compiler_params=_params(2),
        name="mlstm_scan",
    )(proj, proj, proj, proj, gates, g_out, c0, n0, m0)


def _ffn_kernel(x_ref, y_ref, wo_ref, g_ref, wu_ref, wc_ref, bc_ref, wd_ref, prev_ref, gf_ref,
                o_ref, tail_ref, xn_s, acc_s, carry_s, a_s, v_s, *, seq, tm, final_norm):
    sub_len = min(seq, tm)
    n_sub = tm // sub_len
    tiles_per_seq = max(1, seq // tm)
    i = pl.program_id(0)
    starts_seq = (i % tiles_per_seq) == 0

    x = x_ref[...] + _dot(y_ref[...], wo_ref[...])
    xn_s[...] = _rms(x, g_ref[...]).astype(BF16)
    acc_s[...] = x
    row = lax.broadcasted_iota(jnp.int32, (sub_len, 1), 0)

    if n_sub == 1:
        @pl.when(starts_seq)
        def _():
            carry_s[...] = prev_ref[0]

    def up(k):
        c0, n = FFN_SPLITS[k]
        xn = xn_s[...]
        a_s[:, c0:c0 + n] = _dot(xn, wu_ref[:, c0:c0 + n])
        v_s[:, c0:c0 + n] = _dot(xn, wu_ref[:, D_FF + c0:D_FF + c0 + n])

    def down(k):
        c0, n = FFN_SPLITS[k]
        cols = slice(c0, c0 + n)
        w = wc_ref[:, cols]
        b = bc_ref[:, cols]
        hs = []
        for s in range(n_sub):
            rows = slice(s * sub_len, (s + 1) * sub_len)
            a = a_s[rows, cols]
            prev = carry_s[:, cols] if n_sub == 1 else prev_ref[s, :, cols]
            p0, p1 = prev[0:1], prev[1:2]
            am1 = jnp.where(row == 0, p1, pltpu.roll(a, 1, 0))
            am2 = jnp.where(row == 0, p0, jnp.where(row == 1, p1, pltpu.roll(a, 2, 0)))
            conv = ((b + w[0:1] * am2) + w[1:2] * am1) + w[2:3] * a
            hs.append((jax.nn.gelu(conv) * v_s[rows, cols]).astype(BF16))
            tail = a[sub_len - 2:sub_len]
            tail_ref[s, :, cols] = tail
            if n_sub == 1:
                carry_s[:, cols] = tail
        h = hs[0] if n_sub == 1 else jnp.concatenate(hs, axis=0)
        acc_s[...] += _dot(h, wd_ref[cols, :])

    up(0)
    for k in range(len(FFN_SPLITS)):
        if k + 1 < len(FFN_SPLITS):
            up(k + 1)
        down(k)
    out = acc_s[...]
    o_ref[...] = _rms(out, gf_ref[...]) if final_norm else out


def _ffn(x, y, wo, g, w_up, wc, bc, w_dn, prev, g_final, layer, seq, final_norm):
    rows, d = x.shape
    tm = ROW_TILE
    batch = rows // seq
    if seq >= tm:
        seqs_per_tile = 1
        tiles_per_seq = seq // tm
        seq_block = lambda i: (i // tiles_per_seq, 0, 0)
    else:
        seqs_per_tile = tm // seq
        seq_block = lambda i: (i, 0, 0)
    state_spec = pl.BlockSpec((seqs_per_tile, 2, D_FF), seq_block)
    up_spec = pl.BlockSpec((None, d, 2 * D_FF), lambda i: (layer, 0, 0), pipeline_mode=pl.Buffered(1))
    down_spec = pl.BlockSpec((None, D_FF, d), lambda i: (layer, 0, 0), pipeline_mode=pl.Buffered(1))
    return pl.pallas_call(
        functools.partial(_ffn_kernel, seq=seq, tm=tm, final_norm=final_norm),
        grid=(rows // tm,),
        in_specs=[pl.BlockSpec((tm, d), lambda i: (i, 0)), pl.BlockSpec((tm, d), lambda i: (i, 0)),
                  _resident((d, d)), _resident((1, d)), up_spec,
                  _resident((3, D_FF)), _resident((1, D_FF)), down_spec, state_spec,
                  _resident((1, d))],
        out_specs=[pl.BlockSpec((tm, d), lambda i: (i, 0)), state_spec],
        out_shape=[jax.ShapeDtypeStruct((rows, d), F32),
                   jax.ShapeDtypeStruct((batch, 2, D_FF), F32)],
        scratch_shapes=[pltpu.VMEM((tm, d), BF16), pltpu.VMEM((tm, d), F32),
                        pltpu.VMEM((2, D_FF), F32),
                        pltpu.VMEM((tm, D_FF), F32), pltpu.VMEM((tm, D_FF), F32)],
        compiler_params=_params(1),
        name="conv_ffn",
    )(x, y, wo, g, w_up, wc, bc, w_dn, prev, g_final)


def _later_matrix(bk):
    j_idx = lax.broadcasted_iota(jnp.int32, (bk, bk), 0)
    s_idx = lax.broadcasted_iota(jnp.int32, (bk, bk), 1)
    return jnp.where(j_idx > s_idx, 1.0, 0.0).astype(BF16)


def _sb_visit(qs, kbs, vbs, state, later, strict_lower):
    heads = range(len(qs))
    zs = [_dot_nt(qs[h], kbs[h]) for h in heads]
    log_betas, log_keeps = [], []
    for h in heads:
        z = (zs[h] * (SB_HEAD_DIM ** -0.5)).astype(BF16)
        log_beta = _log_sigmoid(z)
        log_keep = log_beta - z
        if strict_lower is not None:
            log_keep = jnp.where(strict_lower, log_keep, jnp.zeros_like(log_keep))
        log_betas.append(log_beta)
        log_keeps.append(log_keep)
    afters = [_dot(log_keeps[h], later) for h in heads]
    weights, runs = [], []
    for h in heads:
        a = jnp.exp(log_betas[h].astype(F32) + afters[h] + state[h][0])
        if strict_lower is not None:
            a = jnp.where(strict_lower, a, 0.0)
        weights.append(a.astype(BF16))
        runs.append(state[h][0] + afters[h][:, 0:1] + log_keeps[h][:, 0:1].astype(F32))
    return tuple((runs[h], state[h][1] + _dot(weights[h], vbs[h])) for h in heads)


def _sb_alive(state):
    top = state[0][0]
    for run, _ in state[1:]:
        top = jnp.maximum(top, run)
    return jnp.max(top) > SB_DEAD_LOG


def _sb_prompt_kernel(x_ref, g_ref, wq_ref, k_ref, v_ref, o_ref, *, blk, heads):
    i = pl.program_id(1)
    dh = SB_HEAD_DIM
    cols = [slice(h * dh, (h + 1) * dh) for h in range(heads)]
    xn = _rms(x_ref[...], g_ref[...]).astype(BF16)
    q_all = _dot(xn, wq_ref[...]).astype(BF16)
    qs = [q_all[:, c] for c in cols]
    t_idx = lax.broadcasted_iota(jnp.int32, (blk, blk), 0)
    s_idx = lax.broadcasted_iota(jnp.int32, (blk, blk), 1)
    later = _later_matrix(blk)

    def visit(off, state, mask):
        return _sb_visit(qs, [k_ref[pl.ds(off, blk), c] for c in cols],
                         [v_ref[pl.ds(off, blk), c] for c in cols], state, later, mask)

    state = tuple((jnp.zeros((blk, 1), F32), jnp.zeros((blk, dh), F32)) for _ in range(heads))
    state = visit(pl.multiple_of(i * blk, blk), state, s_idx < t_idx)

    def cond(carry):
        n, alive, _ = carry
        return jnp.logical_and(n < i, alive)

    def body(carry):
        n, _, state = carry
        state = visit(pl.multiple_of((i - 1 - n) * blk, blk), state, None)
        return n + 1, _sb_alive(state), state

    _, _, state = lax.while_loop(cond, body, (jnp.int32(0), _sb_alive(state), state))
    for h in range(heads):
        o_ref[:, cols[h]] = state[h][1].astype(o_ref.dtype)


def _sb_prompt(x, g, wq, kv, batch, seq):
    blk = SB_BLOCK
    nq = seq // blk
    d = D_MODEL
    return pl.pallas_call(
        functools.partial(_sb_prompt_kernel, blk=blk, heads=SB_HEADS),
        grid=(batch, nq),
        in_specs=[pl.BlockSpec((blk, d), lambda b, i: (b * nq + i, 0)),
                  _resident((1, d)), _resident((d, d)),
                  pl.BlockSpec((seq, d), lambda b, i: (b, 0)),
                  pl.BlockSpec((seq, d), lambda b, i: (b, 1))],
        out_specs=pl.BlockSpec((blk, d), lambda b, i: (b * nq + i, 0)),
        out_shape=jax.ShapeDtypeStruct((batch * seq, d), BF16),
        compiler_params=_params(2),
        name="sb_attention_prompt",
    )(x, g, wq, kv, kv)


def _sb_cached_kernel(q_ref, kv_ref, kc_hbm, vc_hbm, o_ref, kbuf, vbuf, sems, *, seq, blk, n_blocks):
    b = pl.program_id(0)
    dh = SB_HEAD_DIM
    heads = SB_HEADS
    rows = blk * heads
    cols = [slice(h * dh, (h + 1) * dh) for h in range(heads)]
    qs = [q_ref[:, c] for c in cols]

    def copies(stream, n, slot):
        off = pl.multiple_of((n_blocks - 1 - n) * rows, rows)
        return (pltpu.make_async_copy(kc_hbm.at[stream, pl.ds(off, rows), :], kbuf.at[slot],
                                      sems.at[0, slot]),
                pltpu.make_async_copy(vc_hbm.at[stream, pl.ds(off, rows), :], vbuf.at[slot],
                                      sems.at[1, slot]))

    def start(stream, n, slot):
        for c in copies(stream, n, slot):
            c.start()

    def wait(n, slot):
        for c in copies(b, n, slot):
            c.wait()

    @pl.when(b == 0)
    def _():
        start(b, 0, 0)

    t_idx = lax.broadcasted_iota(jnp.int32, (seq, seq), 0)
    s_idx = lax.broadcasted_iota(jnp.int32, (seq, seq), 1)
    state = tuple((jnp.zeros((seq, 1), F32), jnp.zeros((seq, dh), F32)) for _ in range(heads))
    state = _sb_visit(qs, [kv_ref[:, c] for c in cols],
                      [kv_ref[:, D_MODEL + h * dh:D_MODEL + (h + 1) * dh] for h in range(heads)],
                      state, _later_matrix(seq), s_idx < t_idx)
    later = _later_matrix(blk)

    def cond(carry):
        n, alive, _ = carry
        return jnp.logical_and(n < n_blocks, alive)

    def body(carry):
        n, _, state = carry
        slot = n % 2
        wait(n, slot)

        @pl.when(jnp.logical_and(n >= 1, n + 1 < n_blocks))
        def _():
            start(b, n + 1, 1 - slot)

        state = _sb_visit(
            qs, [kbuf[slot, pl.ds(h, blk, stride=heads), :].astype(BF16) for h in range(heads)],
            [vbuf[slot, pl.ds(h, blk, stride=heads), :].astype(BF16) for h in range(heads)],
            state, later, None)
        alive = _sb_alive(state)

        if n_blocks > 1:
            @pl.when(jnp.logical_and(n == 0, alive))
            def _():
                start(b, 1, 1)

        return n + 1, alive, state

    n_done, _, state = lax.while_loop(cond, body, (jnp.int32(0), _sb_alive(state), state))

    @pl.when(jnp.logical_or(n_done == 0, jnp.logical_and(n_done >= 2, n_done < n_blocks)))
    def _():
        wait(n_done, n_done % 2)

    @pl.when(b + 1 < pl.num_programs(0))
    def _():
        start(b + 1, 0, 0)

    for h in range(heads):
        o_ref[:, cols[h]] = state[h][1].astype(o_ref.dtype)


def _sb_cached(q, kv, cache_k, cache_v, batch, seq):
    dh = SB_HEAD_DIM
    blk = SB_BLOCK
    rows = blk * SB_HEADS
    n_blocks = cache_k.shape[1] // rows
    return pl.pallas_call(
        functools.partial(_sb_cached_kernel, seq=seq, blk=blk, n_blocks=n_blocks),
        grid=(batch,),
        in_specs=[pl.BlockSpec((seq, D_MODEL), lambda b: (b, 0)),
                  pl.BlockSpec((seq, 2 * D_MODEL), lambda b: (b, 0)),
                  pl.BlockSpec(memory_space=pl.ANY),
                  pl.BlockSpec(memory_space=pl.ANY)],
        out_specs=pl.BlockSpec((seq, D_MODEL), lambda b: (b, 0)),
        out_shape=jax.ShapeDtypeStruct((batch * seq, D_MODEL), BF16),
        scratch_shapes=[pltpu.VMEM((2, rows, dh), F32), pltpu.VMEM((2, rows, dh), F32),
                        pltpu.SemaphoreType.DMA((2, 2))],
        compiler_params=_params(1),
        name="sb_attention_cached",
    )(q, kv, cache_k, cache_v)


def _prep_weights(g_mix_norm, w_mlstm_in, b_mlstm_gates, g_mlstm_out, w_mlstm_out,
                  g_kv_norm, w_kv, w_sb_q, w_sb_o,
                  g_ffn_norm, w_ffn_up, w_ffn_conv, b_ffn_conv, w_ffn_down, g_final):
    main = 2 * MLSTM_HEADS * MLSTM_QK_DIM + MLSTM_HEADS * MLSTM_V_DIM + D_MODEL
    n_gate = 2 * MLSTM_HEADS
    pad = LANES - n_gate
    return dict(
        g_mix=g_mix_norm.reshape(DEPTH, 1, D_MODEL),
        w_in=w_mlstm_in[:, :, :main].astype(BF16),
        w_gate=jnp.pad(w_mlstm_in[:, :, main:], ((0, 0), (0, 0), (0, pad))).astype(BF16),
        b_gate=jnp.pad(b_mlstm_gates, ((0, 0), (0, pad))).reshape(N_A_LAYERS, 1, LANES),
        g_out=g_mlstm_out.reshape(N_A_LAYERS, 1, D_MODEL),
        w_out=w_mlstm_out.astype(BF16),
        g_kv=g_kv_norm.reshape(1, D_MODEL),
        w_kv=w_kv.astype(BF16),
        w_q=w_sb_q.astype(BF16),
        w_o=w_sb_o.astype(BF16),
        g_ffn=g_ffn_norm.reshape(DEPTH, 1, D_MODEL),
        w_up=w_ffn_up.astype(BF16),
        w_c=w_ffn_conv,
        b_c=b_ffn_conv.reshape(DEPTH, 1, D_FF),
        w_d=w_ffn_down.astype(BF16),
        g_final=g_final.reshape(1, D_MODEL),
    )


def _trunk(x, c0, n0, m0, conv0, cache, w, batch, seq):
    new_c, new_n, new_m, new_conv = [], [], [], []
    kv_bf = k_new = v_new = None
    for l in range(DEPTH):
        if l < N_A_LAYERS:
            proj, gates = _mlstm_in(x, w["g_mix"][l], w["w_in"][l], w["w_gate"][l], w["b_gate"][l])
            mixed, c, n, m = _mlstm_scan(
                proj, gates, w["g_out"][l],
                c0[l].reshape(batch, MLSTM_PAIRS, LANES, LANES),
                jnp.broadcast_to(n0[l].reshape(batch, MLSTM_PAIRS, LANES, 1),
                                 (batch, MLSTM_PAIRS, LANES, LANES)),
                jnp.broadcast_to(m0[l].reshape(batch, MLSTM_HEADS, 1), (batch, MLSTM_HEADS, LANES)),
                batch, seq)
            w_mix = w["w_out"][l]
            new_c.append(c.reshape(batch, MLSTM_HEADS, MLSTM_QK_DIM, MLSTM_V_DIM))
            new_n.append(n[..., 0].reshape(batch, MLSTM_HEADS, MLSTM_QK_DIM))
            new_m.append(m[..., 0])
        else:
            j = l - N_A_LAYERS
            if cache is None:
                mixed = _sb_prompt(x, w["g_mix"][l], w["w_q"][j], kv_bf, batch, seq)
            else:
                (q,) = _norm_matmul(x, w["g_mix"][l], w["w_q"][j], [BF16], "sb_q_proj")
                mixed = _sb_cached(q, kv_bf, cache[0], cache[1], batch, seq)
            w_mix = w["w_o"][j]
        x, tail = _ffn(x, mixed, w_mix, w["g_ffn"][l], w["w_up"], w["w_c"][l], w["b_c"][l],
                       w["w_d"], conv0[l], w["g_final"], l, seq, l == DEPTH - 1)
        new_conv.append(tail)
        if l == N_A_LAYERS - 1:
            k_rows, v_rows, kv_bf = _kv_proj(x, w["g_kv"], w["w_kv"])
            k_new = k_rows.reshape(batch, seq, SB_HEADS, SB_HEAD_DIM)
            v_new = v_rows.reshape(batch, seq, SB_HEADS, SB_HEAD_DIM)
    y = x.reshape(batch, seq, D_MODEL)
    return (y, jnp.stack(new_c), jnp.stack(new_n), jnp.stack(new_m), jnp.stack(new_conv),
            k_new, v_new)


def kernel(x_prompt, x_sample, cache_k, cache_v, state_mlstm_c, state_mlstm_n, state_mlstm_m,
           state_ffn_conv, g_mix_norm, w_mlstm_in, b_mlstm_gates, g_mlstm_out, w_mlstm_out,
           g_kv_norm, w_kv, w_sb_q, w_sb_o, g_ffn_norm, w_ffn_up, w_ffn_conv, b_ffn_conv,
           w_ffn_down, g_final):
    w = _prep_weights(g_mix_norm, w_mlstm_in, b_mlstm_gates, g_mlstm_out, w_mlstm_out,
                      g_kv_norm, w_kv, w_sb_q, w_sb_o,
                      g_ffn_norm, w_ffn_up, w_ffn_conv, b_ffn_conv, w_ffn_down, g_final)
    bp, tp, d = x_prompt.shape
    bs, ts, _ = x_sample.shape
    past = cache_k.shape[1]

    p_c0 = jnp.zeros((N_A_LAYERS, bp, MLSTM_HEADS, MLSTM_QK_DIM, MLSTM_V_DIM), F32)
    p_n0 = jnp.zeros((N_A_LAYERS, bp, MLSTM_HEADS, MLSTM_QK_DIM), F32)
    p_m0 = jnp.full((N_A_LAYERS, bp, MLSTM_HEADS), NEG_BIG, F32)
    p_conv0 = jnp.zeros((DEPTH, bp, 2, D_FF), F32)
    p_out = _trunk(x_prompt.reshape(bp * tp, d), p_c0, p_n0, p_m0, p_conv0, None, w, bp, tp)

    cache = (cache_k.reshape(bs, past * SB_HEADS, SB_HEAD_DIM),
             cache_v.reshape(bs, past * SB_HEADS, SB_HEAD_DIM))
    s_out = _trunk(x_sample.reshape(bs * ts, d), state_mlstm_c, state_mlstm_n, state_mlstm_m,
                   state_ffn_conv, cache, w, bs, ts)
    return (p_out[0], s_out[0]) + p_out[1:] + s_out[1:]
```
